```python
import jax, jax.numpy as jnp
from jax import lax
import numpy as np

D_MODEL = 1024
BATCH = 8
SEQ = 4096
DEPTH = 1
DEC_BATCH = 16
DEC_SEQ = 16
PAST_LEN = 2048

CHUNK = 64
N_MEM = 256
N_XHEADS = 4
XHEAD_DIM = D_MODEL // N_XHEADS
D_CONV = 512
CONV_WIDTH = 31
CONV_BUF = CONV_WIDTH - 1
D_RWKV = 1024
RWKV_HEAD = 64
RWKV_HEADS = D_RWKV // RWKV_HEAD
D_DECAY_LORA = 64
D_AAA_LORA = 64
D_GATE_LORA = 128
N_BRANCH = 2
D_FF = -(-8 * D_MODEL // (3 * 256)) * 256
EPS_RMS = 1e-6
EPS_LN = 1e-5
EPS_GN = RWKV_HEAD * 1e-5
D_SHIFT = 3 * D_RWKV + D_DECAY_LORA + D_AAA_LORA + D_GATE_LORA
OFF_RWKV = 2 * D_CONV
OFF_GATE = OFF_RWKV + D_SHIFT
D_IN = OFF_GATE + N_BRANCH * D_MODEL

kernel_name = "gated_conformer_rwkv7_stream_step"


def _rms(x, g):
    xf = x.astype(jnp.float32)
    y = xf * lax.rsqrt(jnp.mean(xf * xf, -1, keepdims=True) + EPS_RMS) * g.astype(jnp.float32)
    return y.astype(x.dtype)


def _layernorm(x, g, b):
    xf = x.astype(jnp.float32)
    mu = jnp.mean(xf, -1, keepdims=True)
    var = jnp.mean(jnp.square(xf - mu), -1, keepdims=True)
    return ((xf - mu) * lax.rsqrt(var + EPS_LN) * g.astype(jnp.float32) + b.astype(jnp.float32)).astype(x.dtype)


def _wkv7(r, w, k, v, kk, a, s0):
    def step(S, inp):
        r_t, w_t, k_t, v_t, kk_t, a_t = inp
        sa = jnp.einsum('bhvk,bhk->bhv', S, -kk_t)
        S = (S * w_t[:, :, None, :] + sa[..., None] * (kk_t * a_t)[:, :, None, :]
             + v_t[..., None] * k_t[:, :, None, :])
        return S, jnp.einsum('bhvk,bhk->bhv', S, r_t)
    xs = tuple(jnp.moveaxis(t, 1, 0) for t in (r, w, k, v, kk, a))
    s_fin, ys = lax.scan(step, s0, xs)
    return jnp.moveaxis(ys, 0, 1), s_fin


def _mixer(xn, conv_buf, shift_buf, wkv_state, p):
    f32 = jnp.float32
    B, T, _ = xn.shape
    proj = xn @ p["w_in"]
    conv_in = proj[..., :OFF_RWKV]
    pr = proj[..., OFF_RWKV:OFF_GATE]
    gates = jax.nn.sigmoid(proj[..., OFF_GATE:])

    u = conv_in[..., :D_CONV] * jax.nn.sigmoid(conv_in[..., D_CONV:])
    u_pad = jnp.concatenate([conv_buf.astype(u.dtype), u], 1)
    c = lax.conv_general_dilated(u_pad, p["w_dw"][:, None, :], (1,), 'VALID',
                                 dimension_numbers=('NWC', 'WIO', 'NWC'),
                                 feature_group_count=D_CONV) + p["b_dw"]
    c = _layernorm(c, p["ln_conv_g"], p["ln_conv_b"])
    conv_out = jax.nn.silu(c) @ p["w_o_conv"]
    new_conv = u_pad[:, -CONV_BUF:]

    pr_prev = jnp.concatenate([shift_buf.astype(pr.dtype), pr[:, :-1]], 1)
    pm = pr + (pr_prev - pr) * p["mu_shift"]
    new_shift = pr[:, -1:]
    o1 = D_RWKV; o2 = 2 * D_RWKV; o3 = 3 * D_RWKV
    o4 = o3 + D_DECAY_LORA; o5 = o4 + D_AAA_LORA
    r, k, v, wd, ad, gd = jnp.split(pm, [o1, o2, o3, o4, o5], -1)
    w_log = -jax.nn.softplus(-(p["w0"] + jnp.tanh(wd) @ p["w_up"]).astype(f32)) - 0.5
    decay = jnp.exp(-jnp.exp(w_log))
    a = jax.nn.sigmoid((p["a0"] + ad @ p["a_up"]).astype(f32))
    g = jax.nn.sigmoid(gd) @ p["g_up"]
    hs = (B, T, RWKV_HEADS, RWKV_HEAD)
    kf = k.astype(f32)
    kk = (kf * p["k_k"].astype(f32)).reshape(hs)
    kk = kk / jnp.maximum(jnp.sqrt(jnp.sum(kk * kk, -1, keepdims=True)), 1e-12)
    kf = kf * (1.0 + (a - 1.0) * p["k_a"].astype(f32))
    rh = r.astype(f32).reshape(hs)
    kh = kf.reshape(hs)
    vh = v.astype(f32).reshape(hs)
    y, new_wkv = _wkv7(rh, decay.reshape(hs), kh, vh, kk, a.reshape(hs), wkv_state.astype(f32))
    mu = jnp.mean(y, -1, keepdims=True)
    var = jnp.mean(jnp.square(y - mu), -1, keepdims=True)
    y = ((y - mu) * lax.rsqrt(var + EPS_GN)).reshape(B, T, D_RWKV)
    y = y * p["ln_x_g"].astype(f32) + p["ln_x_b"].astype(f32)
    bonus = jnp.sum(rh * kh * p["r_k"].astype(f32), -1, keepdims=True) * vh
    y = y + bonus.reshape(B, T, D_RWKV)
    rwkv_out = (y.astype(xn.dtype) * g) @ p["w_o_rwkv"]

    merged = gates[..., :D_MODEL] * conv_out + gates[..., D_MODEL:] * rwkv_out
    return merged @ p["w_out"], new_conv, new_shift, new_wkv.astype(xn.dtype)


def _mem_kv(mem, g_mem, w_k_mem, w_v_mem):
    B = mem.shape[0]
    mn = _rms(mem, g_mem)
    mk = (mn @ w_k_mem).reshape(B, N_MEM, N_XHEADS, XHEAD_DIM)
    mv = (mn @ w_v_mem).reshape(B, N_MEM, N_XHEADS, XHEAD_DIM)
    return mk, mv


def _cross_attn(xn, mem_k, mem_v, w_q_x, w_o_x):
    B, T, _ = xn.shape
    q = (xn @ w_q_x).reshape(B, T, N_XHEADS, XHEAD_DIM)
    s = jnp.einsum('bthd,bmhd->bhtm', q, mem_k.astype(q.dtype)).astype(jnp.float32) * (XHEAD_DIM ** -0.5)
    pw = jax.nn.softmax(s, -1).astype(xn.dtype)
    o = jnp.einsum('bhtm,bmhd->bthd', pw, mem_v.astype(xn.dtype)).reshape(B, T, D_MODEL)
    return o @ w_o_x


def _swiglu(xn, w_gate, w_up, w_down):
    return (jax.nn.silu(xn @ w_gate) * (xn @ w_up)) @ w_down


def _layer(x, mem_k, mem_v, conv_buf, shift_buf, wkv_state, p):
    mix, new_conv, new_shift, new_wkv = _mixer(_rms(x, p["g_norm_mix"]), conv_buf, shift_buf, wkv_state, p)
    h = x + mix
    h = h + _cross_attn(_rms(h, p["g_norm_x"]), mem_k, mem_v, p["w_q_x"], p["w_o_x"])
    h = h + _swiglu(_rms(h, p["g_norm_ffn"]), p["w_ffn_gate"], p["w_ffn_up"], p["w_ffn_down"])
    return h, new_conv, new_shift, new_wkv


def setup_inputs(seed: int = 0) -> dict:
    key = jax.random.key(seed)
    ks = iter(jax.random.split(key, 64))
    f32 = jnp.float32
    L = DEPTH
    D = D_MODEL

    def nrm(shape, scale):
        return jax.random.normal(next(ks), shape, f32) * scale

    def gain(shape):
        return 1.0 + nrm(shape, 0.02)

    def unif(shape, lo, hi):
        return jax.random.uniform(next(ks), shape, f32, lo, hi)

    return {
        "x_prompt": nrm((BATCH, SEQ, D), 1.0),
        "x_sample": nrm((DEC_BATCH, DEC_SEQ, D), 1.0),
        "cache_mem_k": nrm((L, DEC_BATCH, N_MEM, N_XHEADS, XHEAD_DIM), 1.0),
        "cache_mem_v": nrm((L, DEC_BATCH, N_MEM, N_XHEADS, XHEAD_DIM), 1.0),
        "state_conv": nrm((L, DEC_BATCH, CONV_BUF, D_CONV), 0.5),
        "state_shift": nrm((L, DEC_BATCH, 1, D_SHIFT), 1.0),
        "state_wkv": nrm((L, DEC_BATCH, RWKV_HEADS, RWKV_HEAD, RWKV_HEAD), 1.0),
        "mem_prompt": nrm((BATCH, N_MEM, D), 1.0),
        "g_norm_mix": gain((L, D)),
        "w_in": nrm((L, D, D_IN), D ** -0.5),
        "mu_shift": unif((L, D_SHIFT), 0.0, 1.0),
        "w_dw": nrm((L, CONV_WIDTH, D_CONV), CONV_WIDTH ** -0.5),
        "b_dw": nrm((L, D_CONV), 0.02),
        "ln_conv_g": gain((L, D_CONV)),
        "ln_conv_b": nrm((L, D_CONV), 0.02),
        "w_o_conv": nrm((L, D_CONV, D), D_CONV ** -0.5),
        "w0": unif((L, D_RWKV), -3.0, 1.0),
        "w_up": nrm((L, D_DECAY_LORA, D_RWKV), 0.5 * D_DECAY_LORA ** -0.5),
        "a0": nrm((L, D_RWKV), 0.1),
        "a_up": nrm((L, D_AAA_LORA, D_RWKV), 0.5 * D_AAA_LORA ** -0.5),
        "g_up": nrm((L, D_GATE_LORA, D_RWKV), D_GATE_LORA ** -0.5),
        "k_k": 0.85 + nrm((L, D_RWKV), 0.02),
        "k_a": gain((L, D_RWKV)),
        "r_k": nrm((L, RWKV_HEADS, RWKV_HEAD), 0.1),
        "ln_x_g": gain((L, D_RWKV)),
        "ln_x_b": nrm((L, D_RWKV), 0.02),
        "w_o_rwkv": nrm((L, D_RWKV, D), D_RWKV ** -0.5),
        "w_out": nrm((L, D, D), D ** -0.5),
        "g_norm_x": gain((L, D)),
        "g_mem": gain((L, D)),
        "w_q_x": nrm((L, D, D), D ** -0.5),
        "w_k_mem": nrm((L, D, D), D ** -0.5),
        "w_v_mem": nrm((L, D, D), D ** -0.5),
        "w_o_x": nrm((L, D, D), D ** -0.5),
        "g_norm_ffn": gain((L, D)),
        "w_ffn_gate": nrm((L, D, D_FF), D ** -0.5),
        "w_ffn_up": nrm((L, D, D_FF), D ** -0.5),
        "w_ffn_down": nrm((L, D_FF, D), D_FF ** -0.5),
        "g_norm_final": gain((D,)),
    }


def reference(x_prompt, x_sample, cache_mem_k, cache_mem_v, state_conv, state_shift, state_wkv, mem_prompt,
              g_norm_mix, w_in, mu_shift, w_dw, b_dw, ln_conv_g, ln_conv_b, w_o_conv,
              w0, w_up, a0, a_up, g_up, k_k, k_a, r_k, ln_x_g, ln_x_b, w_o_rwkv, w_out,
              g_norm_x, g_mem, w_q_x, w_k_mem, w_v_mem, w_o_x,
              g_norm_ffn, w_ffn_gate, w_ffn_up, w_ffn_down, g_norm_final):
    Bp = x_prompt.shape[0]
    dt = x_prompt.dtype
    hp, hs = x_prompt, x_sample
    mk_l, mv_l, cp_l, sp_l, wp_l, cs_l, ss_l, ws_l = [], [], [], [], [], [], [], []
    for i in range(DEPTH):
        p = {
            "g_norm_mix": g_norm_mix[i], "w_in": w_in[i], "mu_shift": mu_shift[i],
            "w_dw": w_dw[i], "b_dw": b_dw[i], "ln_conv_g": ln_conv_g[i], "ln_conv_b": ln_conv_b[i],
            "w_o_conv": w_o_conv[i], "w0": w0[i], "w_up": w_up[i], "a0": a0[i], "a_up": a_up[i],
            "g_up": g_up[i], "k_k": k_k[i], "k_a": k_a[i], "r_k": r_k[i], "ln_x_g": ln_x_g[i],
            "ln_x_b": ln_x_b[i], "w_o_rwkv": w_o_rwkv[i], "w_out": w_out[i],
            "g_norm_x": g_norm_x[i], "w_q_x": w_q_x[i], "w_o_x": w_o_x[i],
            "g_norm_ffn": g_norm_ffn[i], "w_ffn_gate": w_ffn_gate[i], "w_ffn_up": w_ffn_up[i],
            "w_ffn_down": w_ffn_down[i],
        }
        mk, mv = _mem_kv(mem_prompt, g_mem[i], w_k_mem[i], w_v_mem[i])
        conv0 = jnp.zeros((Bp, CONV_BUF, D_CONV), dt)
        shift0 = jnp.zeros((Bp, 1, D_SHIFT), dt)
        wkv0 = jnp.zeros((Bp, RWKV_HEADS, RWKV_HEAD, RWKV_HEAD), jnp.float32)
        hp, cp, sp, wp = _layer(hp, mk, mv, conv0, shift0, wkv0, p)
        hs, cs, ss, ws = _layer(hs, cache_mem_k[i], cache_mem_v[i], state_conv[i], state_shift[i],
                                state_wkv[i], p)
        mk_l.append(mk); mv_l.append(mv); cp_l.append(cp); sp_l.append(sp); wp_l.append(wp)
        cs_l.append(cs); ss_l.append(ss); ws_l.append(ws)
    y_prompt = _rms(hp, g_norm_final)
    y_sample = _rms(hs, g_norm_final)
    mem_k_prompt = jnp.stack(mk_l)
    mem_v_prompt = jnp.stack(mv_l)
    conv_prompt = jnp.stack(cp_l)
    shift_prompt = jnp.stack(sp_l)
    wkv_prompt = jnp.stack(wp_l)
    conv_sample = jnp.stack(cs_l)
    shift_sample = jnp.stack(ss_l)
    wkv_sample = jnp.stack(ws_l)
    return (y_prompt, y_sample, mem_k_prompt, mem_v_prompt, conv_prompt, shift_prompt, wkv_prompt,
            conv_sample, shift_sample, wkv_sample)
```

```python
import functools
import math

import jax
import jax.numpy as jnp
from jax import lax
from jax.experimental import pallas as pl
from jax.experimental.pallas import tpu as pltpu

F32 = jnp.float32
BF16 = jnp.bfloat16

D_MODEL = 1024
N_MEM = 256
N_XHEADS = 4
XHEAD_DIM = D_MODEL // N_XHEADS
D_CONV = 512
CONV_WIDTH = 31
CONV_BUF = CONV_WIDTH - 1
CONV_PAD = 32
D_RWKV = 1024
RWKV_HEAD = 64
RWKV_HEADS = D_RWKV // RWKV_HEAD
D_LORA_WA = 128
D_GATE_LORA = 128
D_FF = 2816
EPS_RMS = 1e-6
EPS_LN = 1e-5
EPS_GN = RWKV_HEAD * 1e-5
D_SHIFT = 3 * D_RWKV + D_LORA_WA + D_GATE_LORA
OFF_RWKV = 2 * D_CONV
OFF_GATE = OFF_RWKV + D_SHIFT
D_IN = OFF_GATE + 2 * D_MODEL

LANE_GROUP = 256
HEADS_PER_GROUP = LANE_GROUP // RWKV_HEAD
N_GROUPS = D_RWKV // LANE_GROUP
WKV_CHUNK = 64
VMEM_LIMIT = 56 * 1024 * 1024


def _mm(a, b):
    return jnp.dot(a.astype(BF16), b.astype(BF16), preferred_element_type=F32)


def _mm_nt(a, b):
    return lax.dot_general(a.astype(BF16), b.astype(BF16), (((1,), (1,)), ((), ())), preferred_element_type=F32)


def _mm_tn(a, b):
    return lax.dot_general(a.astype(BF16), b.astype(BF16), (((0,), (0,)), ((), ())), preferred_element_type=F32)


def _rms(x, g):
    return x * lax.rsqrt(jnp.mean(x * x, -1, keepdims=True) + EPS_RMS) * g


def _head_sum(x, ones):
    parts = [_mm(x[:, g * LANE_GROUP:(g + 1) * LANE_GROUP], ones) for g in range(x.shape[1] // LANE_GROUP)]
    return parts[0] if len(parts) == 1 else jnp.concatenate(parts, axis=1)


def _const_spec(shape):
    nd = len(shape)
    return pl.BlockSpec(shape, lambda *_: (0,) * nd, pipeline_mode=pl.Buffered(1))


def _mem_kv_kernel(mem_ref, g_ref, wk_ref, wv_ref, mk_ref, mv_ref):
    mn = _rms(mem_ref[0], g_ref[...]).astype(BF16)
    mk_ref[0] = jnp.dot(mn, wk_ref[...], preferred_element_type=F32)
    mv_ref[0] = jnp.dot(mn, wv_ref[...], preferred_element_type=F32)


def _mem_kv(mem, g_mem, wk, wv):
    B = mem.shape[0]
    blk = pl.BlockSpec((1, N_MEM, D_MODEL), lambda b: (b, 0, 0))
    return pl.pallas_call(
        _mem_kv_kernel,
        grid=(B,),
        in_specs=[blk, _const_spec((1, D_MODEL)), _const_spec((D_MODEL, D_MODEL)), _const_spec((D_MODEL, D_MODEL))],
        out_specs=[blk, blk],
        out_shape=[jax.ShapeDtypeStruct((B, N_MEM, D_MODEL), F32)] * 2,
        compiler_params=pltpu.CompilerParams(dimension_semantics=("arbitrary",), vmem_limit_bytes=VMEM_LIMIT),
        name="mem_kv",
    )(mem, g_mem, wk, wv)


def _mix_pre_kernel(x_ref, cst_ref, sst_ref, gmix_ref, win_ref, mu_ref, wdw_ref, bdw_ref, lng_ref, lnb_ref, woc_ref,
                    w0_ref, wup_ref, a0_ref, aup_ref, gup_ref, kk_ref, ka_ref, rk_ref, ones_ref,
                    r_out, lw_out, k_out, v_out, kkn_out, b_out, g_out, bonus_out, gc_out, g2_out,
                    nconv_out, nshift_out, ubuf, cbuf, prbuf, *, tb):
    t = pl.program_id(1)

    @pl.when(t == 0)
    def _():
        ubuf[0:CONV_PAD, :] = cst_ref[0]
        prbuf[0:8, :] = jnp.broadcast_to(sst_ref[0], (8, D_SHIFT))

    x = x_ref[0]
    n = _rms(x, gmix_ref[...]).astype(BF16)

    cin = jnp.dot(n, win_ref[:, 0:OFF_RWKV], preferred_element_type=F32)
    ubuf[CONV_PAD:CONV_PAD + tb, :] = cin[:, :D_CONV] * jax.nn.sigmoid(cin[:, D_CONV:])
    rows = min(tb, 32)
    base = CONV_PAD - CONV_BUF
    for i in range(tb // rows):
        acc = jnp.broadcast_to(bdw_ref[...], (rows, D_CONV))
        for j in range(CONV_WIDTH):
            acc = acc + wdw_ref[j:j + 1, :] * ubuf[base + i * rows + j: base + i * rows + j + rows, :]
        cbuf[i * rows:(i + 1) * rows, :] = acc
    c = cbuf[...]
    mu = jnp.mean(c, -1, keepdims=True)
    cc = c - mu
    var = jnp.mean(cc * cc, -1, keepdims=True)
    c = cc * lax.rsqrt(var + EPS_LN) * lng_ref[...] + lnb_ref[...]
    conv_out = _mm(c * jax.nn.sigmoid(c), woc_ref[...])
    nconv_out[0] = ubuf[tb + base: tb + CONV_PAD, :]
    ubuf[0:CONV_PAD, :] = ubuf[tb: tb + CONV_PAD, :]

    gates = jax.nn.sigmoid(jnp.dot(n, win_ref[:, OFF_GATE:D_IN], preferred_element_type=F32))
    gc_out[0] = (gates[:, :D_MODEL] * conv_out).astype(BF16)
    g2_out[0] = gates[:, D_MODEL:].astype(BF16)

    pr = jnp.dot(n, win_ref[:, OFF_RWKV:OFF_GATE], preferred_element_type=F32)
    prbuf[8:8 + tb, :] = pr
    pr_prev = prbuf[7:7 + tb, :]
    pm = pr + (pr_prev - pr) * mu_ref[...]
    last = pr[tb - 1:tb, :]
    nshift_out[0] = last
    prbuf[7:8, :] = last

    r = pm[:, 0:D_RWKV]
    k = pm[:, D_RWKV:2 * D_RWKV]
    v = pm[:, 2 * D_RWKV:3 * D_RWKV]
    wa = pm[:, 3 * D_RWKV:3 * D_RWKV + D_LORA_WA]
    gd = pm[:, 3 * D_RWKV + D_LORA_WA:D_SHIFT]
    z = w0_ref[...] + _mm(jnp.tanh(wa), wup_ref[...])
    lw_out[0] = F32(-math.exp(-0.5)) * jax.nn.sigmoid(z)
    a = jax.nn.sigmoid(a0_ref[...] + _mm(wa, aup_ref[...]))
    g_out[0] = _mm(jax.nn.sigmoid(gd), gup_ref[...]).astype(BF16)
    ones = ones_ref[...]
    kk = k * kk_ref[...]
    kk = kk / jnp.maximum(jnp.sqrt(_head_sum(kk * kk, ones)), 1e-12)
    kf = k * (1.0 + (a - 1.0) * ka_ref[...])
    r_out[0] = r
    k_out[0] = kf
    v_out[0] = v.astype(BF16)
    kkn_out[0] = kk
    b_out[0] = kk * a
    bonus_out[0] = _head_sum(r * kf * rk_ref[...], ones) * v


def _mix_pre(x, conv_state, shift_state, w, tb):
    B, T, D = x.shape
    tok = lambda width: pl.BlockSpec((1, tb, width), lambda b, t: (b, t, 0))
    per_b = lambda rows, width: pl.BlockSpec((1, rows, width), lambda b, t: (b, 0, 0))
    consts = [w["g_norm_mix"], w["w_in"], w["mu_shift"], w["w_dw"], w["b_dw"], w["ln_conv_g"], w["ln_conv_b"],
              w["w_o_conv"], w["w0"], w["w_up_pad"], w["a0"], w["a_up_pad"], w["g_up"], w["k_k"], w["k_a"], w["r_k"],
              w["ones_group"]]
    f32_tok = jax.ShapeDtypeStruct((B, T, D_RWKV), F32)
    bf_tok = jax.ShapeDtypeStruct((B, T, D_RWKV), BF16)
    out_shape = [f32_tok, f32_tok, f32_tok, bf_tok, f32_tok, f32_tok, bf_tok, f32_tok, bf_tok, bf_tok,
                 jax.ShapeDtypeStruct((B, CONV_BUF, D_CONV), F32), jax.ShapeDtypeStruct((B, 1, D_SHIFT), F32)]
    out_specs = [tok(D_RWKV)] * 10 + [per_b(CONV_BUF, D_CONV), per_b(1, D_SHIFT)]
    return pl.pallas_call(
        functools.partial(_mix_pre_kernel, tb=tb),
        grid=(B, T // tb),
        in_specs=[tok(D), per_b(CONV_PAD, D_CONV), per_b(1, D_SHIFT)] + [_const_spec(c.shape) for c in consts],
        out_specs=out_specs,
        out_shape=out_shape,
        scratch_shapes=[pltpu.VMEM((CONV_PAD + tb, D_CONV), F32), pltpu.VMEM((tb, D_CONV), F32),
                        pltpu.VMEM((8 + tb, D_SHIFT), F32)],
        compiler_params=pltpu.CompilerParams(dimension_semantics=("arbitrary", "arbitrary"),
                                             vmem_limit_bytes=VMEM_LIMIT),
        name="mix_pre",
    )(x, conv_state, shift_state, *consts)


def _wkv_kernel(*refs, n_chunks, has_state):
    if has_state:
        r_ref, lw_ref, k_ref, v_ref, kk_ref, b_ref, z0_ref, y_ref, zout_ref, z_scr = refs
    else:
        r_ref, lw_ref, k_ref, v_ref, kk_ref, b_ref, y_ref, zout_ref, z_scr = refs
    C, G = WKV_CHUNK, LANE_GROUP
    t = pl.program_id(2)

    @pl.when(t == 0)
    def _():
        z_scr[...] = z0_ref[0, 0] if has_state else jnp.zeros((G, G), F32)

    row = lax.broadcasted_iota(jnp.int32, (G, G), 0)
    col = lax.broadcasted_iota(jnp.int32, (G, G), 1)
    same_head = (row // C) == (col // RWKV_HEAD)
    eye = (row == col).astype(F32)
    trow = lax.broadcasted_iota(jnp.int32, (C, G), 0)
    tcol = lax.broadcasted_iota(jnp.int32, (C, G), 1) % C
    strict = tcol < trow
    incl = tcol <= trow
    eye_w = (tcol == trow).astype(F32)
    tri = (lax.broadcasted_iota(jnp.int32, (C, C), 1) <= lax.broadcasted_iota(jnp.int32, (C, C), 0)).astype(BF16)

    def stack(x):
        return jnp.where(same_head, jnp.concatenate([x] * HEADS_PER_GROUP, axis=0), 0.0).astype(BF16)

    for ci in range(n_chunks):
        sl = slice(ci * C, (ci + 1) * C)
        r, lw, k, kk, b = r_ref[0, sl, :], lw_ref[0, sl, :], k_ref[0, sl, :], kk_ref[0, sl, :], b_ref[0, sl, :]
        v = v_ref[0, sl, :]
        hi = lw.astype(BF16)
        mid = (lw - hi.astype(F32)).astype(BF16)
        lo = (lw - hi.astype(F32) - mid.astype(F32)).astype(BF16)
        L = (jnp.dot(tri, hi, preferred_element_type=F32) + jnp.dot(tri, mid, preferred_element_type=F32)
             + jnp.dot(tri, lo, preferred_element_type=F32))
        Lc = L[C - 1:C, :]
        e_neg = jnp.exp(-L)
        e_end = jnp.exp(Lc - L)
        rt = r * jnp.exp(L)
        kkt = kk * jnp.exp(L - lw)
        kt = k * e_neg
        bt = b * e_neg
        khat = k * e_end
        bhat = b * e_end
        wc = jnp.exp(Lc)

        q2 = jnp.concatenate([kkt, rt], axis=0)
        a_k = _mm_nt(q2, stack(kt))
        a_b = _mm_nt(q2, stack(bt))
        l_k = jnp.where(strict, a_k[:C], 0.0)
        l_b = jnp.where(strict, a_b[:C], 0.0)
        a_rk = jnp.where(incl, a_k[C:], 0.0)
        a_rb = jnp.where(incl, a_b[C:], 0.0)
        xinv = eye_w - l_b
        p = _mm(l_b, stack(l_b))
        n_sq = C.bit_length() - 2
        for j in range(n_sq):
            bd = stack(p)
            if j + 1 < n_sq:
                res = _mm(jnp.concatenate([xinv, p], axis=0), bd)
                xinv = xinv + res[:C]
                p = res[C:]
            else:
                xinv = xinv + _mm(xinv, bd)
        bdv = stack(v)
        lkv = _mm(l_k, bdv)
        ku = _mm(xinv, jnp.concatenate([stack(kkt), stack(lkv)], axis=1))
        kkt_t, uv = ku[:, :G], ku[:, G:]
        ru = _mm(a_rb, jnp.concatenate([stack(kkt_t), stack(uv)], axis=1))
        rt_eff = rt - ru[:, :G]
        yv = _mm(a_rk, bdv) - ru[:, G:]
        g_m = eye * wc - jnp.where(same_head, _mm_tn(bhat, kkt_t), 0.0)
        h_m = jnp.where(same_head, _mm_tn(jnp.concatenate([khat, bhat], axis=0),
                                          jnp.concatenate([v.astype(F32), -uv], axis=0)), 0.0)
        res = _mm(jnp.concatenate([g_m, rt_eff], axis=0), z_scr[...])
        y_ref[0, sl, :] = res[G:] + yv
        z_scr[...] = res[:G] + h_m

    zout_ref[0, 0] = z_scr[...]


def _wkv(r, lw, k, v, kk, b, z0, tbw):
    B, T, _ = r.shape
    has_state = z0 is not None
    tok = pl.BlockSpec((1, tbw, LANE_GROUP), lambda bi, g, t: (bi, t, g))
    zspec = pl.BlockSpec((1, 1, LANE_GROUP, LANE_GROUP), lambda bi, g, t: (bi, g, 0, 0))
    ins = [r, lw, k, v, kk, b] + ([z0] if has_state else [])
    return pl.pallas_call(
        functools.partial(_wkv_kernel, n_chunks=tbw // WKV_CHUNK, has_state=has_state),
        grid=(B, N_GROUPS, T // tbw),
        in_specs=[tok] * 6 + ([zspec] if has_state else []),
        out_specs=[tok, zspec],
        out_shape=[jax.ShapeDtypeStruct((B, T, D_RWKV), F32),
                   jax.ShapeDtypeStruct((B, N_GROUPS, LANE_GROUP, LANE_GROUP), F32)],
        scratch_shapes=[pltpu.VMEM((LANE_GROUP, LANE_GROUP), F32)],
        compiler_params=pltpu.CompilerParams(dimension_semantics=("arbitrary", "arbitrary", "arbitrary"),
                                             vmem_limit_bytes=VMEM_LIMIT),
        name="wkv",
    )(*ins)


def _mix_post_kernel(x_ref, y_ref, bonus_ref, g_ref, gc_ref, g2_ref, mk_ref, mv_ref, lnxg_ref, lnxb_ref, ones_ref,
                     worw_ref, wout_ref, gnx_ref, wq_ref, wox_ref, h_ref):
    ones = ones_ref[...]
    y = y_ref[0]
    inv_n = F32(1.0 / RWKV_HEAD)
    yc = y - _head_sum(y, ones) * inv_n
    var = _head_sum(yc * yc, ones) * inv_n
    yn = yc * lax.rsqrt(var + EPS_GN) * lnxg_ref[...] + lnxb_ref[...] + bonus_ref[0]
    rwkv_out = _mm(yn * g_ref[0].astype(F32), worw_ref[...])
    merged = gc_ref[0].astype(F32) + g2_ref[0].astype(F32) * rwkv_out
    h = x_ref[0] + _mm(merged, wout_ref[...])

    q = _mm(_rms(h, gnx_ref[...]), wq_ref[...])
    attn = jnp.zeros_like(h)
    scale = F32(XHEAD_DIM ** -0.5)
    for hd in range(N_XHEADS):
        cs = slice(hd * XHEAD_DIM, (hd + 1) * XHEAD_DIM)
        s = _mm_nt(q[:, cs], mk_ref[0, :, cs]) * scale
        s = s - jnp.max(s, -1, keepdims=True)
        e = jnp.exp(s)
        pw = e / jnp.sum(e, -1, keepdims=True)
        attn = attn + _mm(_mm(pw, mv_ref[0, :, cs]), wox_ref[cs, :])
    h_ref[0] = h + attn


def _mix_post(x, y, bonus, g, gc, g2, mk, mv, w, tb):
    B, T, D = x.shape
    tok = pl.BlockSpec((1, tb, D), lambda b, t: (b, t, 0))
    mem = pl.BlockSpec((1, N_MEM, D), lambda b, t: (b, 0, 0))
    consts = [w["ln_x_g"], w["ln_x_b"], w["ones_group"], w["w_o_rwkv"], w["w_out"], w["g_norm_x"], w["w_q_x"],
              w["w_o_x"]]
    return pl.pallas_call(
        _mix_post_kernel,
        grid=(B, T // tb),
        in_specs=[tok] * 6 + [mem, mem] + [_const_spec(c.shape) for c in consts],
        out_specs=tok,
        out_shape=jax.ShapeDtypeStruct((B, T, D), F32),
        compiler_params=pltpu.CompilerParams(dimension_semantics=("arbitrary", "arbitrary"),
                                             vmem_limit_bytes=VMEM_LIMIT),
        name="mix_post",
    )(x, y, bonus, g, gc, g2, mk, mv, *consts)


def _ffn_kernel(h_ref, gffn_ref, wg_ref, wu_ref, wd_ref, gfin_ref, y_ref):
    h = h_ref[...]
    xn = _rms(h, gffn_ref[...]).astype(BF16)
    gate = jnp.dot(xn, wg_ref[...], preferred_element_type=F32)
    up = jnp.dot(xn, wu_ref[...], preferred_element_type=F32)
    h = h + _mm(gate * jax.nn.sigmoid(gate) * up, wd_ref[...])
    y_ref[...] = _rms(h, gfin_ref[...])


def _ffn(h, w, tb):
    n_tok, D = h.shape
    tok = pl.BlockSpec((tb, D), lambda i: (i, 0))
    consts = [w["g_norm_ffn"], w["w_ffn_gate"], w["w_ffn_up"], w["w_ffn_down"], w["g_norm_final"]]
    return pl.pallas_call(
        _ffn_kernel,
        grid=(n_tok // tb,),
        in_specs=[tok] + [_const_spec(c.shape) for c in consts],
        out_specs=tok,
        out_shape=jax.ShapeDtypeStruct((n_tok, D), F32),
        compiler_params=pltpu.CompilerParams(dimension_semantics=("arbitrary",), vmem_limit_bytes=VMEM_LIMIT),
        name="ffn",
    )(h, *consts)


def _state_to_blockdiag(s):
    B = s.shape[0]
    st = jnp.swapaxes(s, -1, -2).reshape(B, N_GROUPS, HEADS_PER_GROUP, RWKV_HEAD, RWKV_HEAD)
    eye = jnp.eye(HEADS_PER_GROUP, dtype=s.dtype)
    z = st[:, :, :, :, None, :] * eye[None, None, :, None, :, None]
    return z.reshape(B, N_GROUPS, LANE_GROUP, LANE_GROUP)


def _blockdiag_to_state(z):
    B = z.shape[0]
    z5 = z.reshape(B, N_GROUPS, HEADS_PER_GROUP, RWKV_HEAD, HEADS_PER_GROUP, RWKV_HEAD)
    idx = jnp.arange(HEADS_PER_GROUP)
    blocks = z5[:, :, idx, :, idx, :]
    blocks = jnp.moveaxis(blocks, 0, 2)
    return jnp.swapaxes(blocks, -1, -2).reshape(B, RWKV_HEADS, RWKV_HEAD, RWKV_HEAD)


def _layer(x, mk, mv, conv_state, shift_state, wkv_state, w, tb, tb_ffn):
    B, T, D = x.shape
    conv_pad = jnp.pad(conv_state, ((0, 0), (CONV_PAD - CONV_BUF, 0), (0, 0)))
    (r, lw, k, v, kk, b, g, bonus, gc, g2, new_conv, new_shift) = _mix_pre(x, conv_pad, shift_state, w, tb)
    t_pad = -T % WKV_CHUNK
    ops = [r, lw, k, v, kk, b]
    if t_pad:
        ops = [jnp.pad(o, ((0, 0), (0, t_pad), (0, 0))) for o in ops]
    tbw = min(T + t_pad, 256)
    z0 = None if wkv_state is None else _state_to_blockdiag(wkv_state)
    y, z = _wkv(*ops, z0, tbw)
    if t_pad:
        y = y[:, :T]
    h = _mix_post(x, y, bonus, g, gc, g2, mk, mv, w, tb)
    out = _ffn(h.reshape(B * T, D), w, tb_ffn).reshape(B, T, D)
    return out, new_conv, new_shift, _blockdiag_to_state(z)


def kernel(x_prompt, x_sample, cache_mem_k, cache_mem_v, state_conv, state_shift, state_wkv, mem_prompt, g_norm_mix, w_in, mu_shift, w_dw, b_dw, ln_conv_g, ln_conv_b, w_o_conv, w0, w_up, a0, a_up, g_up, k_k, k_a, r_k, ln_x_g, ln_x_b, w_o_rwkv, w_out, g_norm_x, g_mem, w_q_x, w_k_mem, w_v_mem, w_o_x, g_norm_ffn, w_ffn_gate, w_ffn_up, w_ffn_down, g_norm_final):
    Bp, Tp, D = x_prompt.shape
    Bs, Ts, _ = x_sample.shape
    row = lambda a: a.reshape(1, -1)
    zeros_lora = jnp.zeros((D_LORA_WA // 2, D_RWKV), BF16)
    head_of = jnp.arange(LANE_GROUP) // RWKV_HEAD
    w = {
        "g_norm_mix": row(g_norm_mix[0]), "w_in": w_in[0].astype(BF16), "mu_shift": row(mu_shift[0]),
        "w_dw": w_dw[0], "b_dw": row(b_dw[0]), "ln_conv_g": row(ln_conv_g[0]), "ln_conv_b": row(ln_conv_b[0]),
        "w_o_conv": w_o_conv[0].astype(BF16), "w0": row(w0[0]),
        "w_up_pad": jnp.concatenate([w_up[0].astype(BF16), zeros_lora], 0), "a0": row(a0[0]),
        "a_up_pad": jnp.concatenate([zeros_lora, a_up[0].astype(BF16)], 0), "g_up": g_up[0].astype(BF16),
        "k_k": row(k_k[0]), "k_a": row(k_a[0]), "r_k": row(r_k[0]),
        "ones_group": (head_of[:, None] == head_of[None, :]).astype(BF16),
        "ln_x_g": row(ln_x_g[0]), "ln_x_b": row(ln_x_b[0]), "w_o_rwkv": w_o_rwkv[0].astype(BF16),
        "w_out": w_out[0].astype(BF16), "g_norm_x": row(g_norm_x[0]), "w_q_x": w_q_x[0].astype(BF16),
        "w_o_x": w_o_x[0].astype(BF16), "g_norm_ffn": row(g_norm_ffn[0]),
        "w_ffn_gate": w_ffn_gate[0].astype(BF16), "w_ffn_up": w_ffn_up[0].astype(BF16),
        "w_ffn_down": w_ffn_down[0].astype(BF16), "g_norm_final": row(g_norm_final),
    }
    mk, mv = _mem_kv(mem_prompt, row(g_mem[0]), w_k_mem[0].astype(BF16), w_v_mem[0].astype(BF16))
    conv0 = jnp.zeros((Bp, CONV_BUF, D_CONV), F32)
    shift0 = jnp.zeros((Bp, 1, D_SHIFT), F32)
    tb_p = min(Tp, 256)
    yp, cp, sp, wp = _layer(x_prompt, mk.astype(BF16), mv.astype(BF16), conv0, shift0, None, w, tb_p,
                            min(Bp * Tp, 512))
    mk_s = cache_mem_k[0].reshape(Bs, N_MEM, D).astype(BF16)
    mv_s = cache_mem_v[0].reshape(Bs, N_MEM, D).astype(BF16)
    ys, cs, ss, ws = _layer(x_sample, mk_s, mv_s, state_conv[0], state_shift[0], state_wkv[0], w, Ts,
                            min(Bs * Ts, 512))
    mem_shape = (1, Bp, N_MEM, N_XHEADS, XHEAD_DIM)
    return (yp, ys, mk.reshape(mem_shape), mv.reshape(mem_shape), cp[None], sp[None], wp[None],
            cs[None], ss[None], ws[None])
```

```python
import functools
import math

import jax
import jax.numpy as jnp
from jax import lax
from jax.experimental import pallas as pl
from jax.experimental.pallas import tpu as pltpu

F32 = jnp.float32
BF16 = jnp.bfloat16

D_MODEL = 1024
N_MEM = 256
N_XHEADS = 4
XHEAD_DIM = D_MODEL // N_XHEADS
D_CONV = 512
CONV_WIDTH = 31
CONV_BUF = CONV_WIDTH - 1
CONV_PAD = 32
D_RWKV = 1024
RWKV_HEAD = 64
RWKV_HEADS = D_RWKV // RWKV_HEAD
D_LORA_WA = 128
D_GATE_LORA = 128
D_FF = 2816
EPS_RMS = 1e-6
EPS_LN = 1e-5
EPS_GN = RWKV_HEAD * 1e-5
D_SHIFT = 3 * D_RWKV + D_LORA_WA + D_GATE_LORA
OFF_RWKV = 2 * D_CONV
OFF_GATE = OFF_RWKV + D_SHIFT
D_IN = OFF_GATE + 2 * D_MODEL

LANE_GROUP = 256
HEADS_PER_GROUP = LANE_GROUP // RWKV_HEAD
N_GROUPS = D_RWKV // LANE_GROUP
WKV_CHUNK = 64
VMEM_LIMIT = 56 * 1024 * 1024


def _mm(a, b):
    return jnp.dot(a.astype(BF16), b.astype(BF16), preferred_element_type=F32)


def _mm_nt(a, b):
    return lax.dot_general(a.astype(BF16), b.astype(BF16), (((1,), (1,)), ((), ())), preferred_element_type=F32)


def _mm_tn(a, b):
    return lax.dot_general(a.astype(BF16), b.astype(BF16), (((0,), (0,)), ((), ())), preferred_element_type=F32)


def _sigmoid(x):
    return 0.5 * jnp.tanh(0.5 * x) + 0.5


def _rms(x, g):
    return x * lax.rsqrt(jnp.mean(x * x, -1, keepdims=True) + EPS_RMS) * g


def _head_sum(x, ones):
    parts = [_mm(x[:, g * LANE_GROUP:(g + 1) * LANE_GROUP], ones) for g in range(x.shape[1] // LANE_GROUP)]
    return parts[0] if len(parts) == 1 else jnp.concatenate(parts, axis=1)


def _const_spec(shape):
    nd = len(shape)
    return pl.BlockSpec(shape, lambda *_: (0,) * nd, pipeline_mode=pl.Buffered(1))


def _mem_kv_kernel(mem_ref, g_ref, wk_ref, wv_ref, mk_ref, mv_ref):
    mn = _rms(mem_ref[0], g_ref[...]).astype(BF16)
    mk_ref[0] = jnp.dot(mn, wk_ref[...], preferred_element_type=F32)
    mv_ref[0] = jnp.dot(mn, wv_ref[...], preferred_element_type=F32)


def _mem_kv(mem, g_mem, wk, wv):
    B = mem.shape[0]
    blk = pl.BlockSpec((1, N_MEM, D_MODEL), lambda b: (b, 0, 0))
    return pl.pallas_call(
        _mem_kv_kernel,
        grid=(B,),
        in_specs=[blk, _const_spec((1, D_MODEL)), _const_spec((D_MODEL, D_MODEL)), _const_spec((D_MODEL, D_MODEL))],
        out_specs=[blk, blk],
        out_shape=[jax.ShapeDtypeStruct((B, N_MEM, D_MODEL), F32)] * 2,
        compiler_params=pltpu.CompilerParams(dimension_semantics=("arbitrary",), vmem_limit_bytes=VMEM_LIMIT),
        name="mem_kv",
    )(mem, g_mem, wk, wv)


def _mix_pre_kernel(x_ref, cst_ref, sst_ref, gmix_ref, win_ref, mu_ref, wdw_ref, bdw_ref, lng_ref, lnb_ref, woc_ref,
                    w0_ref, wup_ref, a0_ref, aup_ref, gup_ref, kk_ref, ka_ref, rk_ref, ones_ref,
                    r_out, lw_out, k_out, v_out, kkn_out, b_out, g_out, bonus_out, gc_out, g2_out,
                    nconv_out, nshift_out, ubuf, sbuf, cbuf, prbuf, *, tb):
    t = pl.program_id(1)

    @pl.when(t == 0)
    def _():
        ubuf[0:CONV_PAD, :] = cst_ref[0]
        prbuf[0:8, :] = jnp.broadcast_to(sst_ref[0], (8, D_SHIFT))

    x = x_ref[0]
    n = _rms(x, gmix_ref[...]).astype(BF16)

    cin = jnp.dot(n, win_ref[:, 0:OFF_RWKV], preferred_element_type=F32)
    ubuf[CONV_PAD:CONV_PAD + tb, :] = cin[:, :D_CONV] * _sigmoid(cin[:, D_CONV:])
    base = CONV_PAD - CONV_BUF
    span = tb + CONV_PAD - 8
    for s in range(1, 8):
        sbuf[s - 1] = ubuf[s:s + span, :]
    rows = min(tb, 32)
    for i in range(tb // rows):
        acc = jnp.broadcast_to(bdw_ref[...], (rows, D_CONV))
        for j in range(CONV_WIDTH):
            q, s = divmod(j + base, 8)
            off = 8 * q + i * rows
            tap = ubuf[off:off + rows, :] if s == 0 else sbuf[s - 1, off:off + rows, :]
            acc = acc + wdw_ref[j:j + 1, :] * tap
        cbuf[i * rows:(i + 1) * rows, :] = acc
    c = cbuf[...]
    mu = jnp.mean(c, -1, keepdims=True)
    cc = c - mu
    var = jnp.mean(cc * cc, -1, keepdims=True)
    c = cc * lax.rsqrt(var + EPS_LN) * lng_ref[...] + lnb_ref[...]
    conv_out = _mm(c * _sigmoid(c), woc_ref[...])
    nconv_out[0] = ubuf[tb + base: tb + CONV_PAD, :]
    ubuf[0:CONV_PAD, :] = ubuf[tb: tb + CONV_PAD, :]

    gates = _sigmoid(jnp.dot(n, win_ref[:, OFF_GATE:D_IN], preferred_element_type=F32))
    gc_out[0] = (gates[:, :D_MODEL] * conv_out).astype(BF16)
    g2_out[0] = gates[:, D_MODEL:].astype(BF16)

    pr = jnp.dot(n, win_ref[:, OFF_RWKV:OFF_GATE], preferred_element_type=F32)
    prbuf[8:8 + tb, :] = pr
    pr_prev = prbuf[7:7 + tb, :]
    pm = pr + (pr_prev - pr) * mu_ref[...]
    last = pr[tb - 1:tb, :]
    nshift_out[0] = last
    prbuf[7:8, :] = last

    r = pm[:, 0:D_RWKV]
    k = pm[:, D_RWKV:2 * D_RWKV]
    v = pm[:, 2 * D_RWKV:3 * D_RWKV]
    wa = pm[:, 3 * D_RWKV:3 * D_RWKV + D_LORA_WA]
    gd = pm[:, 3 * D_RWKV + D_LORA_WA:D_SHIFT]
    z = w0_ref[...] + _mm(jnp.tanh(wa), wup_ref[...])
    lw_out[0] = F32(-math.exp(-0.5)) * _sigmoid(z)
    a = _sigmoid(a0_ref[...] + _mm(wa, aup_ref[...]))
    g_out[0] = _mm(_sigmoid(gd), gup_ref[...]).astype(BF16)
    ones = ones_ref[...]
    kk = k * kk_ref[...]
    kk = kk * jnp.minimum(lax.rsqrt(_head_sum(kk * kk, ones)), 1e12)
    kf = k * (1.0 + (a - 1.0) * ka_ref[...])
    r_out[0] = r
    k_out[0] = kf
    v_out[0] = v.astype(BF16)
    kkn_out[0] = kk
    b_out[0] = kk * a
    bonus_out[0] = _head_sum(r * kf * rk_ref[...], ones) * v


def _mix_pre(x, conv_state, shift_state, w, tb):
    B, T, D = x.shape
    tok = lambda width: pl.BlockSpec((1, tb, width), lambda b, t: (b, t, 0))
    per_b = lambda rows, width: pl.BlockSpec((1, rows, width), lambda b, t: (b, 0, 0))
    consts = [w["g_norm_mix"], w["w_in"], w["mu_shift"], w["w_dw"], w["b_dw"], w["ln_conv_g"], w["ln_conv_b"],
              w["w_o_conv"], w["w0"], w["w_up_pad"], w["a0"], w["a_up_pad"], w["g_up"], w["k_k"], w["k_a"], w["r_k"],
              w["ones_group"]]
    f32_tok = jax.ShapeDtypeStruct((B, T, D_RWKV), F32)
    bf_tok = jax.ShapeDtypeStruct((B, T, D_RWKV), BF16)
    out_shape = [f32_tok, f32_tok, f32_tok, bf_tok, f32_tok, f32_tok, bf_tok, f32_tok, bf_tok, bf_tok,
                 jax.ShapeDtypeStruct((B, CONV_BUF, D_CONV), F32), jax.ShapeDtypeStruct((B, 1, D_SHIFT), F32)]
    out_specs = [tok(D_RWKV)] * 10 + [per_b(CONV_BUF, D_CONV), per_b(1, D_SHIFT)]
    return pl.pallas_call(
        functools.partial(_mix_pre_kernel, tb=tb),
        grid=(B, T // tb),
        in_specs=[tok(D), per_b(CONV_PAD, D_CONV), per_b(1, D_SHIFT)] + [_const_spec(c.shape) for c in consts],
        out_specs=out_specs,
        out_shape=out_shape,
        scratch_shapes=[pltpu.VMEM((CONV_PAD + tb, D_CONV), F32), pltpu.VMEM((7, tb + CONV_PAD - 8, D_CONV), F32),
                        pltpu.VMEM((tb, D_CONV), F32),
                        pltpu.VMEM((8 + tb, D_SHIFT), F32)],
        compiler_params=pltpu.CompilerParams(dimension_semantics=("arbitrary", "arbitrary"),
                                             vmem_limit_bytes=VMEM_LIMIT),
        name="mix_pre",
    )(x, conv_state, shift_state, *consts)


def _wkv_kernel(*refs, n_chunks, has_state):
    if has_state:
        r_ref, lw_ref, k_ref, v_ref, kk_ref, b_ref, z0_ref, y_ref, zout_ref, z_scr = refs
    else:
        r_ref, lw_ref, k_ref, v_ref, kk_ref, b_ref, y_ref, zout_ref, z_scr = refs
    C, G = WKV_CHUNK, LANE_GROUP
    groups = range(N_GROUPS)

    @pl.when(pl.program_id(1) == 0)
    def _():
        z_scr[...] = z0_ref[0] if has_state else jnp.zeros((N_GROUPS, G, G), F32)

    row = lax.broadcasted_iota(jnp.int32, (G, G), 0)
    col = lax.broadcasted_iota(jnp.int32, (G, G), 1)
    same_head = (row // C) == (col // RWKV_HEAD)
    eye = (row == col).astype(F32)
    trow = lax.broadcasted_iota(jnp.int32, (C, G), 0)
    tcol = lax.broadcasted_iota(jnp.int32, (C, G), 1) % C
    strict = tcol < trow
    incl = tcol <= trow
    eye_w = (tcol == trow).astype(F32)
    tri = (lax.broadcasted_iota(jnp.int32, (C, 3 * C), 1) % C
           <= lax.broadcasted_iota(jnp.int32, (C, 3 * C), 0)).astype(BF16)

    def stack(x):
        xb = x.astype(BF16)
        return jnp.where(same_head, jnp.concatenate([xb] * HEADS_PER_GROUP, axis=0), jnp.zeros((), BF16))

    def cumsum(x):
        hi = x.astype(BF16)
        mid = (x - hi.astype(F32)).astype(BF16)
        lo = (x - hi.astype(F32) - mid.astype(F32)).astype(BF16)
        return jnp.dot(tri, jnp.concatenate([hi, mid, lo], axis=0), preferred_element_type=F32)

    def chunk(ci, carry):
        rows = pl.ds(pl.multiple_of(ci * C, C), C)
        load = lambda ref: [ref[0, rows, g * G:(g + 1) * G] for g in groups]
        r, lw, k, v, kk, b = (load(ref) for ref in (r_ref, lw_ref, k_ref, v_ref, kk_ref, b_ref))
        L = [cumsum(x) for x in lw]
        Lc = [x[C - 1:C, :] for x in L]
        e_neg = [jnp.exp(-x) for x in L]
        e_end = [jnp.exp(c - x) for c, x in zip(Lc, L)]
        rt = [x * jnp.exp(l) for x, l in zip(r, L)]
        kkt = [x * jnp.exp(l - w) for x, l, w in zip(kk, L, lw)]
        q2 = [jnp.concatenate([a, c], axis=0).astype(BF16) for a, c in zip(kkt, rt)]
        a_k = [_mm_nt(q, stack(x * e)) for q, x, e in zip(q2, k, e_neg)]
        a_b = [_mm_nt(q, stack(x * e)) for q, x, e in zip(q2, b, e_neg)]
        l_k = [jnp.where(strict, a[:C], 0.0) for a in a_k]
        l_b = [jnp.where(strict, a[:C], 0.0) for a in a_b]
        a_rk = [jnp.where(incl, a[C:], 0.0) for a in a_k]
        a_rb = [jnp.where(incl, a[C:], 0.0) for a in a_b]
        xinv = [eye_w - a for a in l_b]
        p = [_mm(a, stack(a)) for a in l_b]
        n_sq = C.bit_length() - 2
        for j in range(n_sq):
            bd = [stack(a) for a in p]
            if j + 1 < n_sq:
                res = [_mm(jnp.concatenate([x, a], axis=0), m) for x, a, m in zip(xinv, p, bd)]
                xinv = [x + a[:C] for x, a in zip(xinv, res)]
                p = [a[C:] for a in res]
            else:
                xinv = [x + _mm(x, m) for x, m in zip(xinv, bd)]
        bdv = [stack(x) for x in v]
        lkv = [_mm(a, m) for a, m in zip(l_k, bdv)]
        ku = [_mm(x, jnp.concatenate([stack(a), stack(c)], axis=1)) for x, a, c in zip(xinv, kkt, lkv)]
        ru = [_mm(a, jnp.concatenate([stack(x[:, :G]), stack(x[:, G:])], axis=1)) for a, x in zip(a_rb, ku)]
        yv = [_mm(a, m) - x[:, G:] for a, m, x in zip(a_rk, bdv, ru)]
        lhs1 = [jnp.concatenate([x[:, :G], c - a[:, :G]], axis=0).astype(BF16) for x, c, a in zip(ku, rt, ru)]
        lhs2 = [jnp.concatenate([x * e, c * e, eye * jnp.exp(lc)], axis=0).astype(BF16)
                for x, c, e, lc in zip(k, b, e_end, Lc)]
        z = [z_scr[g] for g in groups]
        res = [_mm(a, x) for a, x in zip(lhs1, z)]
        u = [-(a[:C] + x[:, G:]) for a, x in zip(res, ku)]
        for g in groups:
            y_ref[0, rows, g * G:(g + 1) * G] = res[g][C:] + yv[g]
        znew = [_mm_tn(a, jnp.concatenate([c.astype(F32), x, s], axis=0)) for a, c, x, s in zip(lhs2, v, u, z)]
        for g in groups:
            z_scr[g] = jnp.where(same_head, znew[g], 0.0)
        return carry

    if n_chunks == 1:
        chunk(0, 0)
    else:
        lax.fori_loop(0, n_chunks, chunk, 0)
    zout_ref[0] = z_scr[...]


def _wkv(r, lw, k, v, kk, b, z0, tbw):
    B, T, _ = r.shape
    has_state = z0 is not None
    tok = pl.BlockSpec((1, tbw, D_RWKV), lambda bi, t: (bi, t, 0))
    zspec = pl.BlockSpec((1, N_GROUPS, LANE_GROUP, LANE_GROUP), lambda bi, t: (bi, 0, 0, 0))
    ins = [r, lw, k, v, kk, b] + ([z0] if has_state else [])
    return pl.pallas_call(
        functools.partial(_wkv_kernel, n_chunks=tbw // WKV_CHUNK, has_state=has_state),
        grid=(B, T // tbw),
        in_specs=[tok] * 6 + ([zspec] if has_state else []),
        out_specs=[tok, zspec],
        out_shape=[jax.ShapeDtypeStruct((B, T, D_RWKV), F32),
                   jax.ShapeDtypeStruct((B, N_GROUPS, LANE_GROUP, LANE_GROUP), F32)],
        scratch_shapes=[pltpu.VMEM((N_GROUPS, LANE_GROUP, LANE_GROUP), F32)],
        compiler_params=pltpu.CompilerParams(dimension_semantics=("arbitrary", "arbitrary"),
                                             vmem_limit_bytes=VMEM_LIMIT),
        name="wkv",
    )(*ins)


def _mix_post_kernel(x_ref, y_ref, bonus_ref, g_ref, gc_ref, g2_ref, mk_ref, mv_ref, lnxg_ref, lnxb_ref, ones_ref,
                     worw_ref, wout_ref, gnx_ref, wq_ref, wox_ref, h_ref):
    ones = ones_ref[...]
    y = y_ref[0]
    inv_n = F32(1.0 / RWKV_HEAD)
    yc = y - _head_sum(y, ones) * inv_n
    var = _head_sum(yc * yc, ones) * inv_n
    yn = yc * lax.rsqrt(var + EPS_GN) * lnxg_ref[...] + lnxb_ref[...] + bonus_ref[0]
    rwkv_out = _mm(yn * g_ref[0].astype(F32), worw_ref[...])
    merged = gc_ref[0].astype(F32) + g2_ref[0].astype(F32) * rwkv_out
    h = x_ref[0] + _mm(merged, wout_ref[...])

    q = _mm(_rms(h, gnx_ref[...]), wq_ref[...])
    attn = jnp.zeros_like(h)
    scale = F32(XHEAD_DIM ** -0.5)
    for hd in range(N_XHEADS):
        cs = slice(hd * XHEAD_DIM, (hd + 1) * XHEAD_DIM)
        s = _mm_nt(q[:, cs], mk_ref[0, :, cs]) * scale
        s = s - jnp.max(s, -1, keepdims=True)
        e = jnp.exp(s)
        pw = e * (1.0 / jnp.sum(e, -1, keepdims=True))
        attn = attn + _mm(_mm(pw, mv_ref[0, :, cs]), wox_ref[cs, :])
    h_ref[0] = h + attn


def _mix_post(x, y, bonus, g, gc, g2, mk, mv, w, tb):
    B, T, D = x.shape
    tok = pl.BlockSpec((1, tb, D), lambda b, t: (b, t, 0))
    mem = pl.BlockSpec((1, N_MEM, D), lambda b, t: (b, 0, 0))
    consts = [w["ln_x_g"], w["ln_x_b"], w["ones_group"], w["w_o_rwkv"], w["w_out"], w["g_norm_x"], w["w_q_x"],
              w["w_o_x"]]
    return pl.pallas_call(
        _mix_post_kernel,
        grid=(B, T // tb),
        in_specs=[tok] * 6 + [mem, mem] + [_const_spec(c.shape) for c in consts],
        out_specs=tok,
        out_shape=jax.ShapeDtypeStruct((B, T, D), F32),
        compiler_params=pltpu.CompilerParams(dimension_semantics=("arbitrary", "arbitrary"),
                                             vmem_limit_bytes=VMEM_LIMIT),
        name="mix_post",
    )(x, y, bonus, g, gc, g2, mk, mv, *consts)


def _ffn_kernel(h_ref, gffn_ref, wg_ref, wu_ref, wd_ref, gfin_ref, y_ref):
    h = h_ref[...]
    xn = _rms(h, gffn_ref[...]).astype(BF16)
    gate = jnp.dot(xn, wg_ref[...], preferred_element_type=F32)
    up = jnp.dot(xn, wu_ref[...], preferred_element_type=F32)
    h = h + _mm(gate * _sigmoid(gate) * up, wd_ref[...])
    y_ref[...] = _rms(h, gfin_ref[...])


def _ffn(h, w, tb):
    n_tok, D = h.shape
    tok = pl.BlockSpec((tb, D), lambda i: (i, 0))
    consts = [w["g_norm_ffn"], w["w_ffn_gate"], w["w_ffn_up"], w["w_ffn_down"], w["g_norm_final"]]
    return pl.pallas_call(
        _ffn_kernel,
        grid=(n_tok // tb,),
        in_specs=[tok] + [_const_spec(c.shape) for c in consts],
        out_specs=tok,
        out_shape=jax.ShapeDtypeStruct((n_tok, D), F32),
        compiler_params=pltpu.CompilerParams(dimension_semantics=("arbitrary",), vmem_limit_bytes=VMEM_LIMIT),
        name="ffn",
    )(h, *consts)


def _state_to_blockdiag(s):
    B = s.shape[0]
    st = jnp.swapaxes(s, -1, -2).reshape(B, N_GROUPS, HEADS_PER_GROUP, RWKV_HEAD, RWKV_HEAD)
    eye = jnp.eye(HEADS_PER_GROUP, dtype=s.dtype)
    z = st[:, :, :, :, None, :] * eye[None, None, :, None, :, None]
    return z.reshape(B, N_GROUPS, LANE_GROUP, LANE_GROUP)


def _blockdiag_to_state(z):
    B = z.shape[0]
    z5 = z.reshape(B, N_GROUPS, HEADS_PER_GROUP, RWKV_HEAD, HEADS_PER_GROUP, RWKV_HEAD)
    idx = jnp.arange(HEADS_PER_GROUP)
    blocks = z5[:, :, idx, :, idx, :]
    blocks = jnp.moveaxis(blocks, 0, 2)
    return jnp.swapaxes(blocks, -1, -2).reshape(B, RWKV_HEADS, RWKV_HEAD, RWKV_HEAD)


def _layer(x, mk, mv, conv_state, shift_state, wkv_state, w, tb, tb_ffn):
    B, T, D = x.shape
    conv_pad = jnp.pad(conv_state, ((0, 0), (CONV_PAD - CONV_BUF, 0), (0, 0)))
    (r, lw, k, v, kk, b, g, bonus, gc, g2, new_conv, new_shift) = _mix_pre(x, conv_pad, shift_state, w, tb)
    t_pad = -T % WKV_CHUNK
    ops = [r, lw, k, v, kk, b]
    if t_pad:
        ops = [jnp.pad(o, ((0, 0), (0, t_pad), (0, 0))) for o in ops]
    tbw = min(T + t_pad, 512)
    z0 = None if wkv_state is None else _state_to_blockdiag(wkv_state)
    y, z = _wkv(*ops, z0, tbw)
    if t_pad:
        y = y[:, :T]
    h = _mix_post(x, y, bonus, g, gc, g2, mk, mv, w, tb)
    out = _ffn(h.reshape(B * T, D), w, tb_ffn).reshape(B, T, D)
    return out, new_conv, new_shift, _blockdiag_to_state(z)


def kernel(x_prompt, x_sample, cache_mem_k, cache_mem_v, state_conv, state_shift, state_wkv, mem_prompt, g_norm_mix, w_in, mu_shift, w_dw, b_dw, ln_conv_g, ln_conv_b, w_o_conv, w0, w_up, a0, a_up, g_up, k_k, k_a, r_k, ln_x_g, ln_x_b, w_o_rwkv, w_out, g_norm_x, g_mem, w_q_x, w_k_mem, w_v_mem, w_o_x, g_norm_ffn, w_ffn_gate, w_ffn_up, w_ffn_down, g_norm_final):
    Bp, Tp, D = x_prompt.shape
    Bs, Ts, _ = x_sample.shape
    row = lambda a: a.reshape(1, -1)
    zeros_lora = jnp.zeros((D_LORA_WA // 2, D_RWKV), BF16)
    head_of = jnp.arange(LANE_GROUP) // RWKV_HEAD
    w = {
        "g_norm_mix": row(g_norm_mix[0]), "w_in": w_in[0].astype(BF16), "mu_shift": row(mu_shift[0]),
        "w_dw": w_dw[0], "b_dw": row(b_dw[0]), "ln_conv_g": row(ln_conv_g[0]), "ln_conv_b": row(ln_conv_b[0]),
        "w_o_conv": w_o_conv[0].astype(BF16), "w0": row(w0[0]),
        "w_up_pad": jnp.concatenate([w_up[0].astype(BF16), zeros_lora], 0), "a0": row(a0[0]),
        "a_up_pad": jnp.concatenate([zeros_lora, a_up[0].astype(BF16)], 0), "g_up": g_up[0].astype(BF16),
        "k_k": row(k_k[0]), "k_a": row(k_a[0]), "r_k": row(r_k[0]),
        "ones_group": (head_of[:, None] == head_of[None, :]).astype(BF16),
        "ln_x_g": row(ln_x_g[0]), "ln_x_b": row(ln_x_b[0]), "w_o_rwkv": w_o_rwkv[0].astype(BF16),
        "w_out": w_out[0].astype(BF16), "g_norm_x": row(g_norm_x[0]), "w_q_x": w_q_x[0].astype(BF16),
        "w_o_x": w_o_x[0].astype(BF16), "g_norm_ffn": row(g_norm_ffn[0]),
        "w_ffn_gate": w_ffn_gate[0].astype(BF16), "w_ffn_up": w_ffn_up[0].astype(BF16),
        "w_ffn_down": w_ffn_down[0].astype(BF16), "g_norm_final": row(g_norm_final),
    }
    mk, mv = _mem_kv(mem_prompt, row(g_mem[0]), w_k_mem[0].astype(BF16), w_v_mem[0].astype(BF16))
    conv0 = jnp.zeros((Bp, CONV_BUF, D_CONV), F32)
    shift0 = jnp.zeros((Bp, 1, D_SHIFT), F32)
    tb_p = min(Tp, 256)
    yp, cp, sp, wp = _layer(x_prompt, mk.astype(BF16), mv.astype(BF16), conv0, shift0, None, w, tb_p,
                            min(Bp * Tp, 512))
    mk_s = cache_mem_k[0].reshape(Bs, N_MEM, D).astype(BF16)
    mv_s = cache_mem_v[0].reshape(Bs, N_MEM, D).astype(BF16)
    ys, cs, ss, ws = _layer(x_sample, mk_s, mv_s, state_conv[0], state_shift[0], state_wkv[0], w, Ts,
                            min(Bs * Ts, 512))
    mem_shape = (1, Bp, N_MEM, N_XHEADS, XHEAD_DIM)
    return (yp, ys, mk.reshape(mem_shape), mv.reshape(mem_shape), cp[None], sp[None], wp[None],
            cs[None], ss[None], ws[None])
```

```python
import functools
import math

import jax
import jax.numpy as jnp
from jax import lax
from jax.experimental import pallas as pl
from jax.experimental.pallas import tpu as pltpu

F32 = jnp.float32
BF16 = jnp.bfloat16

D_MODEL = 1024
N_MEM = 256
N_XHEADS = 4
XHEAD_DIM = D_MODEL // N_XHEADS
D_CONV = 512
CONV_WIDTH = 31
CONV_BUF = CONV_WIDTH - 1
CONV_PAD = 32
D_RWKV = 1024
RWKV_HEAD = 64
RWKV_HEADS = D_RWKV // RWKV_HEAD
D_LORA_WA = 128
D_GATE_LORA = 128
D_FF = 2816
EPS_RMS = 1e-6
EPS_LN = 1e-5
EPS_GN = RWKV_HEAD * 1e-5
D_SHIFT = 3 * D_RWKV + D_LORA_WA + D_GATE_LORA
OFF_RWKV = 2 * D_CONV
OFF_GATE = OFF_RWKV + D_SHIFT
D_IN = OFF_GATE + 2 * D_MODEL

LANE_GROUP = 256
HEADS_PER_GROUP = LANE_GROUP // RWKV_HEAD
N_GROUPS = D_RWKV // LANE_GROUP
WKV_CHUNK = 64
VMEM_LIMIT = 56 * 1024 * 1024


def _mm(a, b):
    return jnp.dot(a.astype(BF16), b.astype(BF16), preferred_element_type=F32)


def _mm_nt(a, b):
    return lax.dot_general(a.astype(BF16), b.astype(BF16), (((1,), (1,)), ((), ())), preferred_element_type=F32)


def _mm_tn(a, b):
    return lax.dot_general(a.astype(BF16), b.astype(BF16), (((0,), (0,)), ((), ())), preferred_element_type=F32)


def _sigmoid(x):
    return 0.5 * jnp.tanh(0.5 * x) + 0.5


def _rms(x, g):
    return x * lax.rsqrt(jnp.mean(x * x, -1, keepdims=True) + EPS_RMS) * g


def _head_sum(x, ones):
    parts = [_mm(x[:, g * LANE_GROUP:(g + 1) * LANE_GROUP], ones) for g in range(x.shape[1] // LANE_GROUP)]
    return parts[0] if len(parts) == 1 else jnp.concatenate(parts, axis=1)


def _const_spec(shape):
    nd = len(shape)
    return pl.BlockSpec(shape, lambda *_: (0,) * nd, pipeline_mode=pl.Buffered(1))


def _mem_kv_kernel(mem_ref, g_ref, wk_ref, wv_ref, mk_ref, mv_ref):
    mn = _rms(mem_ref[0], g_ref[...]).astype(BF16)
    mk_ref[0] = jnp.dot(mn, wk_ref[...], preferred_element_type=F32)
    mv_ref[0] = jnp.dot(mn, wv_ref[...], preferred_element_type=F32)


def _mem_kv(mem, g_mem, wk, wv):
    B = mem.shape[0]
    blk = pl.BlockSpec((1, N_MEM, D_MODEL), lambda b: (b, 0, 0))
    return pl.pallas_call(
        _mem_kv_kernel,
        grid=(B,),
        in_specs=[blk, _const_spec((1, D_MODEL)), _const_spec((D_MODEL, D_MODEL)), _const_spec((D_MODEL, D_MODEL))],
        out_specs=[blk, blk],
        out_shape=[jax.ShapeDtypeStruct((B, N_MEM, D_MODEL), F32)] * 2,
        compiler_params=pltpu.CompilerParams(dimension_semantics=("arbitrary",), vmem_limit_bytes=VMEM_LIMIT),
        name="mem_kv",
    )(mem, g_mem, wk, wv)


def _mix_pre_kernel(x_ref, cst_ref, sst_ref, gmix_ref, win_ref, mu_ref, wdw_ref, bdw_ref, lng_ref, lnb_ref, woc_ref,
                    w0_ref, wup_ref, a0_ref, aup_ref, gup_ref, kk_ref, ka_ref, rk_ref, ones_ref,
                    r_out, lw_out, k_out, v_out, kkn_out, b_out, g_out, bonus_out, gc_out, g2_out,
                    nconv_out, nshift_out, ubuf, sbuf, cbuf, prbuf, *, tb):
    t = pl.program_id(1)

    @pl.when(t == 0)
    def _():
        ubuf[0:CONV_PAD, :] = cst_ref[0]
        prbuf[0:8, :] = jnp.broadcast_to(sst_ref[0], (8, D_SHIFT))

    x = x_ref[0]
    n = _rms(x, gmix_ref[...]).astype(BF16)

    cin = jnp.dot(n, win_ref[:, 0:OFF_RWKV], preferred_element_type=F32)
    ubuf[CONV_PAD:CONV_PAD + tb, :] = cin[:, :D_CONV] * _sigmoid(cin[:, D_CONV:])
    base = CONV_PAD - CONV_BUF
    span = tb + CONV_PAD - 8
    for s in range(1, 8):
        sbuf[s - 1] = ubuf[s:s + span, :]
    rows = min(tb, 32)
    for i in range(tb // rows):
        acc = jnp.broadcast_to(bdw_ref[...], (rows // 8, 8, D_CONV))
        for j in range(CONV_WIDTH):
            q, s = divmod(j + base, 8)
            off = 8 * q + i * rows
            tap = ubuf[off:off + rows, :] if s == 0 else sbuf[s - 1, off:off + rows, :]
            acc = acc + wdw_ref[8 * j:8 * j + 8, :][None] * tap.reshape(rows // 8, 8, D_CONV)
        cbuf[i * rows:(i + 1) * rows, :] = acc.reshape(rows, D_CONV)
    c = cbuf[...]
    mu = jnp.mean(c, -1, keepdims=True)
    cc = c - mu
    var = jnp.mean(cc * cc, -1, keepdims=True)
    c = cc * lax.rsqrt(var + EPS_LN) * lng_ref[...] + lnb_ref[...]
    conv_out = _mm(c * _sigmoid(c), woc_ref[...])
    nconv_out[0] = ubuf[tb + base: tb + CONV_PAD, :]
    ubuf[0:CONV_PAD, :] = ubuf[tb: tb + CONV_PAD, :]

    gates = _sigmoid(jnp.dot(n, win_ref[:, OFF_GATE:D_IN], preferred_element_type=F32))
    gc_out[0] = (gates[:, :D_MODEL] * conv_out).astype(BF16)
    g2_out[0] = gates[:, D_MODEL:].astype(BF16)

    pr = jnp.dot(n, win_ref[:, OFF_RWKV:OFF_GATE], preferred_element_type=F32)
    prbuf[8:8 + tb, :] = pr
    pr_prev = prbuf[7:7 + tb, :]
    pm = pr + (pr_prev - pr) * mu_ref[...]
    last = pr[tb - 1:tb, :]
    nshift_out[0] = last
    prbuf[7:8, :] = last

    r = pm[:, 0:D_RWKV]
    k = pm[:, D_RWKV:2 * D_RWKV]
    v = pm[:, 2 * D_RWKV:3 * D_RWKV]
    wa = pm[:, 3 * D_RWKV:3 * D_RWKV + D_LORA_WA]
    gd = pm[:, 3 * D_RWKV + D_LORA_WA:D_SHIFT]
    z = w0_ref[...] + _mm(jnp.tanh(wa), wup_ref[...])
    lw_out[0] = F32(-math.exp(-0.5)) * _sigmoid(z)
    a = _sigmoid(a0_ref[...] + _mm(wa, aup_ref[...]))
    g_out[0] = _mm(_sigmoid(gd), gup_ref[...]).astype(BF16)
    ones = ones_ref[...]
    kk = k * kk_ref[...]
    kk = kk * jnp.minimum(lax.rsqrt(_head_sum(kk * kk, ones)), 1e12)
    kf = k * (1.0 + (a - 1.0) * ka_ref[...])
    r_out[0] = r
    k_out[0] = kf
    v_out[0] = v.astype(BF16)
    kkn_out[0] = kk
    b_out[0] = kk * a
    bonus_out[0] = _head_sum(r * kf * rk_ref[...], ones) * v


def _mix_pre(x, conv_state, shift_state, w, tb):
    B, T, D = x.shape
    tok = lambda width: pl.BlockSpec((1, tb, width), lambda b, t: (b, t, 0))
    per_b = lambda rows, width: pl.BlockSpec((1, rows, width), lambda b, t: (b, 0, 0))
    consts = [w["g_norm_mix"], w["w_in"], w["mu_shift"], w["w_dw"], w["b_dw"], w["ln_conv_g"], w["ln_conv_b"],
              w["w_o_conv"], w["w0"], w["w_up_pad"], w["a0"], w["a_up_pad"], w["g_up"], w["k_k"], w["k_a"], w["r_k"],
              w["ones_group"]]
    f32_tok = jax.ShapeDtypeStruct((B, T, D_RWKV), F32)
    bf_tok = jax.ShapeDtypeStruct((B, T, D_RWKV), BF16)
    out_shape = [f32_tok, f32_tok, f32_tok, bf_tok, f32_tok, f32_tok, bf_tok, f32_tok, bf_tok, bf_tok,
                 jax.ShapeDtypeStruct((B, CONV_BUF, D_CONV), F32), jax.ShapeDtypeStruct((B, 1, D_SHIFT), F32)]
    out_specs = [tok(D_RWKV)] * 10 + [per_b(CONV_BUF, D_CONV), per_b(1, D_SHIFT)]
    return pl.pallas_call(
        functools.partial(_mix_pre_kernel, tb=tb),
        grid=(B, T // tb),
        in_specs=[tok(D), per_b(CONV_PAD, D_CONV), per_b(1, D_SHIFT)] + [_const_spec(c.shape) for c in consts],
        out_specs=out_specs,
        out_shape=out_shape,
        scratch_shapes=[pltpu.VMEM((CONV_PAD + tb, D_CONV), F32), pltpu.VMEM((7, tb + CONV_PAD - 8, D_CONV), F32),
                        pltpu.VMEM((tb, D_CONV), F32),
                        pltpu.VMEM((8 + tb, D_SHIFT), F32)],
        compiler_params=pltpu.CompilerParams(dimension_semantics=("arbitrary", "arbitrary"),
                                             vmem_limit_bytes=VMEM_LIMIT),
        name="mix_pre",
    )(x, conv_state, shift_state, *consts)


def _wkv_kernel(*refs, n_chunks, has_state):
    if has_state:
        r_ref, lw_ref, k_ref, v_ref, kk_ref, b_ref, z0_ref, y_ref, zout_ref, z_scr = refs
    else:
        r_ref, lw_ref, k_ref, v_ref, kk_ref, b_ref, y_ref, zout_ref, z_scr = refs
    C, G = WKV_CHUNK, LANE_GROUP
    groups = range(N_GROUPS)

    @pl.when(pl.program_id(1) == 0)
    def _():
        z_scr[...] = z0_ref[0] if has_state else jnp.zeros((N_GROUPS, G, G), F32)

    row = lax.broadcasted_iota(jnp.int32, (G, G), 0)
    col = lax.broadcasted_iota(jnp.int32, (G, G), 1)
    same_head = (row // C) == (col // RWKV_HEAD)
    eye = (row == col).astype(F32)
    trow = lax.broadcasted_iota(jnp.int32, (C, G), 0)
    tcol = lax.broadcasted_iota(jnp.int32, (C, G), 1) % C
    strict = tcol < trow
    incl = tcol <= trow
    eye_w = (tcol == trow).astype(F32)
    tri = (lax.broadcasted_iota(jnp.int32, (C, 3 * C), 1) % C
           <= lax.broadcasted_iota(jnp.int32, (C, 3 * C), 0)).astype(BF16)

    def stack(x):
        xb = x.astype(BF16)
        return jnp.where(same_head, jnp.concatenate([xb] * HEADS_PER_GROUP, axis=0), jnp.zeros((), BF16))

    def cumsum(x):
        hi = x.astype(BF16)
        mid = (x - hi.astype(F32)).astype(BF16)
        lo = (x - hi.astype(F32) - mid.astype(F32)).astype(BF16)
        return jnp.dot(tri, jnp.concatenate([hi, mid, lo], axis=0), preferred_element_type=F32)

    n_sub = 2 if n_chunks % 2 == 0 else 1

    def chunk(ci, carry):
        rows = [pl.ds(pl.multiple_of((ci * n_sub + s) * C, C), C) for s in range(n_sub)]
        load = lambda ref: [ref[0, rows[s], g * G:(g + 1) * G] for s in range(n_sub) for g in groups]
        r, lw, k, v, kk, b = (load(ref) for ref in (r_ref, lw_ref, k_ref, v_ref, kk_ref, b_ref))
        L = [cumsum(x) for x in lw]
        Lc = [x[C - 1:C, :] for x in L]
        e_neg = [jnp.exp(-x) for x in L]
        e_end = [jnp.exp(c - x) for c, x in zip(Lc, L)]
        rt = [x * jnp.exp(l) for x, l in zip(r, L)]
        kkt = [x * jnp.exp(l - w) for x, l, w in zip(kk, L, lw)]
        q2 = [jnp.concatenate([a, c], axis=0).astype(BF16) for a, c in zip(kkt, rt)]
        a_k = [_mm_nt(q, stack(x * e)) for q, x, e in zip(q2, k, e_neg)]
        a_b = [_mm_nt(q, stack(x * e)) for q, x, e in zip(q2, b, e_neg)]
        l_k = [jnp.where(strict, a[:C], 0.0) for a in a_k]
        l_b = [jnp.where(strict, a[:C], 0.0) for a in a_b]
        a_rk = [jnp.where(incl, a[C:], 0.0) for a in a_k]
        a_rb = [jnp.where(incl, a[C:], 0.0) for a in a_b]
        xinv = [eye_w - a for a in l_b]
        p = [_mm(a, stack(a)) for a in l_b]
        n_sq = C.bit_length() - 2
        for j in range(n_sq):
            bd = [stack(a) for a in p]
            if j + 1 < n_sq:
                res = [_mm(jnp.concatenate([x, a], axis=0), m) for x, a, m in zip(xinv, p, bd)]
                xinv = [x + a[:C] for x, a in zip(xinv, res)]
                p = [a[C:] for a in res]
            else:
                xinv = [x + _mm(x, m) for x, m in zip(xinv, bd)]
        bdv = [stack(x) for x in v]
        lkv = [_mm(a, m) for a, m in zip(l_k, bdv)]
        ku = [_mm(x, jnp.concatenate([stack(a), stack(c)], axis=1)) for x, a, c in zip(xinv, kkt, lkv)]
        ru = [_mm(a, jnp.concatenate([stack(x[:, :G]), stack(x[:, G:])], axis=1)) for a, x in zip(a_rb, ku)]
        yv = [_mm(a, m) - x[:, G:] for a, m, x in zip(a_rk, bdv, ru)]
        lhs1 = [jnp.concatenate([x[:, :G], c - a[:, :G]], axis=0).astype(BF16) for x, c, a in zip(ku, rt, ru)]
        lhs2 = [jnp.concatenate([x * e, c * e, eye * jnp.exp(lc)], axis=0).astype(BF16)
                for x, c, e, lc in zip(k, b, e_end, Lc)]
        z = [z_scr[g] for g in groups]
        for s in range(n_sub):
            sl = slice(s * N_GROUPS, (s + 1) * N_GROUPS)
            res = [_mm(a, x) for a, x in zip(lhs1[sl], z)]
            u = [-(a[:C] + x[:, G:]) for a, x in zip(res, ku[sl])]
            for g in groups:
                y_ref[0, rows[s], g * G:(g + 1) * G] = res[g][C:] + yv[sl][g]
            znew = [_mm_tn(a, jnp.concatenate([c.astype(F32), x, zz], axis=0))
                    for a, c, x, zz in zip(lhs2[sl], v[sl], u, z)]
            z = [jnp.where(same_head, a, 0.0) for a in znew]
        for g in groups:
            z_scr[g] = z[g]
        return carry

    if n_chunks == n_sub:
        chunk(0, 0)
    else:
        lax.fori_loop(0, n_chunks // n_sub, chunk, 0)
    zout_ref[0] = z_scr[...]


def _wkv(r, lw, k, v, kk, b, z0, tbw):
    B, T, _ = r.shape
    has_state = z0 is not None
    tok = pl.BlockSpec((1, tbw, D_RWKV), lambda bi, t: (bi, t, 0))
    zspec = pl.BlockSpec((1, N_GROUPS, LANE_GROUP, LANE_GROUP), lambda bi, t: (bi, 0, 0, 0))
    ins = [r, lw, k, v, kk, b] + ([z0] if has_state else [])
    return pl.pallas_call(
        functools.partial(_wkv_kernel, n_chunks=tbw // WKV_CHUNK, has_state=has_state),
        grid=(B, T // tbw),
        in_specs=[tok] * 6 + ([zspec] if has_state else []),
        out_specs=[tok, zspec],
        out_shape=[jax.ShapeDtypeStruct((B, T, D_RWKV), F32),
                   jax.ShapeDtypeStruct((B, N_GROUPS, LANE_GROUP, LANE_GROUP), F32)],
        scratch_shapes=[pltpu.VMEM((N_GROUPS, LANE_GROUP, LANE_GROUP), F32)],
        compiler_params=pltpu.CompilerParams(dimension_semantics=("arbitrary", "arbitrary"),
                                             vmem_limit_bytes=VMEM_LIMIT),
        name="wkv",
    )(*ins)


def _mix_post_kernel(x_ref, y_ref, bonus_ref, g_ref, gc_ref, g2_ref, mk_ref, mv_ref, lnxg_ref, lnxb_ref, ones_ref,
                     worw_ref, wout_ref, gnx_ref, wq_ref, wox_ref, h_ref):
    ones = ones_ref[...]
    y = y_ref[0]
    inv_n = F32(1.0 / RWKV_HEAD)
    yc = y - _head_sum(y, ones) * inv_n
    var = _head_sum(yc * yc, ones) * inv_n
    yn = yc * lax.rsqrt(var + EPS_GN) * lnxg_ref[...] + lnxb_ref[...] + bonus_ref[0]
    rwkv_out = _mm(yn * g_ref[0].astype(F32), worw_ref[...])
    merged = gc_ref[0].astype(F32) + g2_ref[0].astype(F32) * rwkv_out
    h = x_ref[0] + _mm(merged, wout_ref[...])

    q = _mm(_rms(h, gnx_ref[...]), wq_ref[...])
    attn = jnp.zeros_like(h)
    scale = F32(XHEAD_DIM ** -0.5)
    for hd in range(N_XHEADS):
        cs = slice(hd * XHEAD_DIM, (hd + 1) * XHEAD_DIM)
        s = _mm_nt(q[:, cs], mk_ref[0, :, cs]) * scale
        s = s - jnp.max(s, -1, keepdims=True)
        e = jnp.exp(s)
        pw = e * (1.0 / jnp.sum(e, -1, keepdims=True))
        attn = attn + _mm(_mm(pw, mv_ref[0, :, cs]), wox_ref[cs, :])
    h_ref[0] = h + attn


def _mix_post(x, y, bonus, g, gc, g2, mk, mv, w, tb):
    B, T, D = x.shape
    tok = pl.BlockSpec((1, tb, D), lambda b, t: (b, t, 0))
    mem = pl.BlockSpec((1, N_MEM, D), lambda b, t: (b, 0, 0))
    consts = [w["ln_x_g"], w["ln_x_b"], w["ones_group"], w["w_o_rwkv"], w["w_out"], w["g_norm_x"], w["w_q_x"],
              w["w_o_x"]]
    return pl.pallas_call(
        _mix_post_kernel,
        grid=(B, T // tb),
        in_specs=[tok] * 6 + [mem, mem] + [_const_spec(c.shape) for c in consts],
        out_specs=tok,
        out_shape=jax.ShapeDtypeStruct((B, T, D), F32),
        compiler_params=pltpu.CompilerParams(dimension_semantics=("arbitrary", "arbitrary"),
                                             vmem_limit_bytes=VMEM_LIMIT),
        name="mix_post",
    )(x, y, bonus, g, gc, g2, mk, mv, *consts)


def _ffn_kernel(h_ref, gffn_ref, wg_ref, wu_ref, wd_ref, gfin_ref, y_ref):
    h = h_ref[...]
    xn = _rms(h, gffn_ref[...]).astype(BF16)
    gate = jnp.dot(xn, wg_ref[...], preferred_element_type=F32)
    up = jnp.dot(xn, wu_ref[...], preferred_element_type=F32)
    h = h + _mm(gate * _sigmoid(gate) * up, wd_ref[...])
    y_ref[...] = _rms(h, gfin_ref[...])


def _ffn(h, w, tb):
    n_tok, D = h.shape
    tok = pl.BlockSpec((tb, D), lambda i: (i, 0))
    consts = [w["g_norm_ffn"], w["w_ffn_gate"], w["w_ffn_up"], w["w_ffn_down"], w["g_norm_final"]]
    return pl.pallas_call(
        _ffn_kernel,
        grid=(n_tok // tb,),
        in_specs=[tok] + [_const_spec(c.shape) for c in consts],
        out_specs=tok,
        out_shape=jax.ShapeDtypeStruct((n_tok, D), F32),
        compiler_params=pltpu.CompilerParams(dimension_semantics=("arbitrary",), vmem_limit_bytes=VMEM_LIMIT),
        name="ffn",
    )(h, *consts)


def _state_to_blockdiag(s):
    B = s.shape[0]
    st = jnp.swapaxes(s, -1, -2).reshape(B, N_GROUPS, HEADS_PER_GROUP, RWKV_HEAD, RWKV_HEAD)
    eye = jnp.eye(HEADS_PER_GROUP, dtype=s.dtype)
    z = st[:, :, :, :, None, :] * eye[None, None, :, None, :, None]
    return z.reshape(B, N_GROUPS, LANE_GROUP, LANE_GROUP)


def _blockdiag_to_state(z):
    B = z.shape[0]
    z5 = z.reshape(B, N_GROUPS, HEADS_PER_GROUP, RWKV_HEAD, HEADS_PER_GROUP, RWKV_HEAD)
    idx = jnp.arange(HEADS_PER_GROUP)
    blocks = z5[:, :, idx, :, idx, :]
    blocks = jnp.moveaxis(blocks, 0, 2)
    return jnp.swapaxes(blocks, -1, -2).reshape(B, RWKV_HEADS, RWKV_HEAD, RWKV_HEAD)


def _layer(x, mk, mv, conv_state, shift_state, wkv_state, w, tb, tb_ffn):
    B, T, D = x.shape
    conv_pad = jnp.pad(conv_state, ((0, 0), (CONV_PAD - CONV_BUF, 0), (0, 0)))
    (r, lw, k, v, kk, b, g, bonus, gc, g2, new_conv, new_shift) = _mix_pre(x, conv_pad, shift_state, w, tb)
    t_pad = -T % WKV_CHUNK
    ops = [r, lw, k, v, kk, b]
    if t_pad:
        ops = [jnp.pad(o, ((0, 0), (0, t_pad), (0, 0))) for o in ops]
    tbw = min(T + t_pad, 512)
    z0 = None if wkv_state is None else _state_to_blockdiag(wkv_state)
    y, z = _wkv(*ops, z0, tbw)
    if t_pad:
        y = y[:, :T]
    h = _mix_post(x, y, bonus, g, gc, g2, mk, mv, w, min(T, 2 * tb))
    out = _ffn(h.reshape(B * T, D), w, tb_ffn).reshape(B, T, D)
    return out, new_conv, new_shift, _blockdiag_to_state(z)


def kernel(x_prompt, x_sample, cache_mem_k, cache_mem_v, state_conv, state_shift, state_wkv, mem_prompt, g_norm_mix, w_in, mu_shift, w_dw, b_dw, ln_conv_g, ln_conv_b, w_o_conv, w0, w_up, a0, a_up, g_up, k_k, k_a, r_k, ln_x_g, ln_x_b, w_o_rwkv, w_out, g_norm_x, g_mem, w_q_x, w_k_mem, w_v_mem, w_o_x, g_norm_ffn, w_ffn_gate, w_ffn_up, w_ffn_down, g_norm_final):
    Bp, Tp, D = x_prompt.shape
    Bs, Ts, _ = x_sample.shape
    row = lambda a: a.reshape(1, -1)
    zeros_lora = jnp.zeros((D_LORA_WA // 2, D_RWKV), BF16)
    head_of = jnp.arange(LANE_GROUP) // RWKV_HEAD
    w = {
        "g_norm_mix": row(g_norm_mix[0]), "w_in": w_in[0].astype(BF16), "mu_shift": row(mu_shift[0]),
        "w_dw": jnp.repeat(w_dw[0], 8, axis=0), "b_dw": row(b_dw[0]), "ln_conv_g": row(ln_conv_g[0]), "ln_conv_b": row(ln_conv_b[0]),
        "w_o_conv": w_o_conv[0].astype(BF16), "w0": row(w0[0]),
        "w_up_pad": jnp.concatenate([w_up[0].astype(BF16), zeros_lora], 0), "a0": row(a0[0]),
        "a_up_pad": jnp.concatenate([zeros_lora, a_up[0].astype(BF16)], 0), "g_up": g_up[0].astype(BF16),
        "k_k": row(k_k[0]), "k_a": row(k_a[0]), "r_k": row(r_k[0]),
        "ones_group": (head_of[:, None] == head_of[None, :]).astype(BF16),
        "ln_x_g": row(ln_x_g[0]), "ln_x_b": row(ln_x_b[0]), "w_o_rwkv": w_o_rwkv[0].astype(BF16),
        "w_out": w_out[0].astype(BF16), "g_norm_x": row(g_norm_x[0]), "w_q_x": w_q_x[0].astype(BF16),
        "w_o_x": w_o_x[0].astype(BF16), "g_norm_ffn": row(g_norm_ffn[0]),
        "w_ffn_gate": w_ffn_gate[0].astype(BF16), "w_ffn_up": w_ffn_up[0].astype(BF16),
        "w_ffn_down": w_ffn_down[0].astype(BF16), "g_norm_final": row(g_norm_final),
    }
    mk, mv = _mem_kv(mem_prompt, row(g_mem[0]), w_k_mem[0].astype(BF16), w_v_mem[0].astype(BF16))
    conv0 = jnp.zeros((Bp, CONV_BUF, D_CONV), F32)
    shift0 = jnp.zeros((Bp, 1, D_SHIFT), F32)
    tb_p = min(Tp, 256)
    yp, cp, sp, wp = _layer(x_prompt, mk.astype(BF16), mv.astype(BF16), conv0, shift0, None, w, tb_p,
                            min(Bp * Tp, 512))
    mk_s = cache_mem_k[0].reshape(Bs, N_MEM, D).astype(BF16)
    mv_s = cache_mem_v[0].reshape(Bs, N_MEM, D).astype(BF16)
    ys, cs, ss, ws = _layer(x_sample, mk_s, mv_s, state_conv[0], state_shift[0], state_wkv[0], w, Ts,
                            min(Bs * Ts, 512))
    mem_shape = (1, Bp, N_MEM, N_XHEADS, XHEAD_DIM)
    return (yp, ys, mk.reshape(mem_shape), mv.reshape(mem_shape), cp[None], sp[None], wp[None],
            cs[None], ss[None], ws[None])
```

```python
import functools
import math

import jax
import jax.numpy as jnp
from jax import lax
from jax.experimental import pallas as pl
from jax.experimental.pallas import tpu as pltpu

F32 = jnp.float32
BF16 = jnp.bfloat16

D_MODEL = 1024
N_MEM = 256
N_XHEADS = 4
XHEAD_DIM = D_MODEL // N_XHEADS
D_CONV = 512
CONV_WIDTH = 31
CONV_BUF = CONV_WIDTH - 1
CONV_PAD = 32
D_RWKV = 1024
RWKV_HEAD = 64
RWKV_HEADS = D_RWKV // RWKV_HEAD
D_LORA_WA = 128
D_GATE_LORA = 128
D_FF = 2816
EPS_RMS = 1e-6
EPS_LN = 1e-5
EPS_GN = RWKV_HEAD * 1e-5
D_SHIFT = 3 * D_RWKV + D_LORA_WA + D_GATE_LORA
OFF_RWKV = 2 * D_CONV
OFF_GATE = OFF_RWKV + D_SHIFT
D_IN = OFF_GATE + 2 * D_MODEL

LANE_GROUP = 256
HEADS_PER_GROUP = LANE_GROUP // RWKV_HEAD
N_GROUPS = D_RWKV // LANE_GROUP
WKV_CHUNK = 64
VMEM_LIMIT = 56 * 1024 * 1024


def _mm(a, b):
    return jnp.dot(a.astype(BF16), b.astype(BF16), preferred_element_type=F32)


def _mm_nt(a, b):
    return lax.dot_general(a.astype(BF16), b.astype(BF16), (((1,), (1,)), ((), ())), preferred_element_type=F32)


def _mm_tn(a, b):
    return lax.dot_general(a.astype(BF16), b.astype(BF16), (((0,), (0,)), ((), ())), preferred_element_type=F32)


def _sigmoid(x):
    return 0.5 * jnp.tanh(0.5 * x) + 0.5


def _rms(x, g):
    return x * lax.rsqrt(jnp.mean(x * x, -1, keepdims=True) + EPS_RMS) * g


def _head_sum(x, ones):
    parts = [_mm(x[:, g * LANE_GROUP:(g + 1) * LANE_GROUP], ones) for g in range(x.shape[1] // LANE_GROUP)]
    return parts[0] if len(parts) == 1 else jnp.concatenate(parts, axis=1)


def _const_spec(shape):
    nd = len(shape)
    return pl.BlockSpec(shape, lambda *_: (0,) * nd, pipeline_mode=pl.Buffered(1))


def _mem_kv_kernel(mem_ref, g_ref, wk_ref, wv_ref, mk_ref, mv_ref):
    mn = _rms(mem_ref[0], g_ref[...]).astype(BF16)
    mk_ref[0] = jnp.dot(mn, wk_ref[...], preferred_element_type=F32)
    mv_ref[0] = jnp.dot(mn, wv_ref[...], preferred_element_type=F32)


def _mem_kv(mem, g_mem, wk, wv):
    B = mem.shape[0]
    blk = pl.BlockSpec((1, N_MEM, D_MODEL), lambda b: (b, 0, 0))
    return pl.pallas_call(
        _mem_kv_kernel,
        grid=(B,),
        in_specs=[blk, _const_spec((1, D_MODEL)), _const_spec((D_MODEL, D_MODEL)), _const_spec((D_MODEL, D_MODEL))],
        out_specs=[blk, blk],
        out_shape=[jax.ShapeDtypeStruct((B, N_MEM, D_MODEL), F32)] * 2,
        compiler_params=pltpu.CompilerParams(dimension_semantics=("arbitrary",), vmem_limit_bytes=VMEM_LIMIT),
        name="mem_kv",
    )(mem, g_mem, wk, wv)


def _mix_pre_kernel(x_ref, cst_ref, sst_ref, gmix_ref, win_ref, mu_ref, wdw_ref, bdw_ref, lng_ref, lnb_ref, woc_ref,
                    w0_ref, wup_ref, a0_ref, aup_ref, gup_ref, kk_ref, ka_ref, rk_ref, ones_ref,
                    r_out, lw_out, k_out, v_out, kkn_out, b_out, g_out, bonus_out, gc_out, g2_out,
                    nconv_out, nshift_out, ubuf, sbuf, cbuf, prbuf, *, tb):
    t = pl.program_id(1)

    @pl.when(t == 0)
    def _():
        ubuf[0:CONV_PAD, :] = cst_ref[0]
        prbuf[0:8, :] = jnp.broadcast_to(sst_ref[0], (8, D_SHIFT))

    x = x_ref[0]
    n = _rms(x, gmix_ref[...]).astype(BF16)

    cin = jnp.dot(n, win_ref[:, 0:OFF_RWKV], preferred_element_type=F32)
    ubuf[CONV_PAD:CONV_PAD + tb, :] = cin[:, :D_CONV] * _sigmoid(cin[:, D_CONV:])
    base = CONV_PAD - CONV_BUF
    span = tb + CONV_PAD - 8
    for s in range(1, 8):
        sbuf[s - 1] = ubuf[s:s + span, :]
    rows = min(tb, 32)
    n_blocks = tb // rows
    pr_chunk = 256
    n_pr_chunks = D_SHIFT // pr_chunk
    pr_done = 0
    for i in range(n_blocks):
        acc = jnp.broadcast_to(bdw_ref[...], (rows // 8, 8, D_CONV))
        for j in range(CONV_WIDTH):
            q, s = divmod(j + base, 8)
            off = 8 * q + i * rows
            tap = ubuf[off:off + rows, :] if s == 0 else sbuf[s - 1, off:off + rows, :]
            acc = acc + wdw_ref[8 * j:8 * j + 8, :][None] * tap.reshape(rows // 8, 8, D_CONV)
        cbuf[i * rows:(i + 1) * rows, :] = acc.reshape(rows, D_CONV)
        pr_upto = (i + 1) * n_pr_chunks // n_blocks
        for ch in range(pr_done, pr_upto):
            c0 = ch * pr_chunk
            prbuf[8:8 + tb, c0:c0 + pr_chunk] = jnp.dot(n, win_ref[:, OFF_RWKV + c0:OFF_RWKV + c0 + pr_chunk],
                                                        preferred_element_type=F32)
        pr_done = pr_upto
    c = cbuf[...]
    mu = jnp.mean(c, -1, keepdims=True)
    cc = c - mu
    var = jnp.mean(cc * cc, -1, keepdims=True)
    c = cc * lax.rsqrt(var + EPS_LN) * lng_ref[...] + lnb_ref[...]
    conv_out = _mm(c * _sigmoid(c), woc_ref[...])
    nconv_out[0] = ubuf[tb + base: tb + CONV_PAD, :]
    ubuf[0:CONV_PAD, :] = ubuf[tb: tb + CONV_PAD, :]

    gates = _sigmoid(jnp.dot(n, win_ref[:, OFF_GATE:D_IN], preferred_element_type=F32))
    gc_out[0] = (gates[:, :D_MODEL] * conv_out).astype(BF16)
    g2_out[0] = gates[:, D_MODEL:].astype(BF16)

    pr = prbuf[8:8 + tb, :]
    pr_prev = prbuf[7:7 + tb, :]
    pm = pr + (pr_prev - pr) * mu_ref[...]
    last = pr[tb - 1:tb, :]
    nshift_out[0] = last
    prbuf[7:8, :] = last

    r = pm[:, 0:D_RWKV]
    k = pm[:, D_RWKV:2 * D_RWKV]
    v = pm[:, 2 * D_RWKV:3 * D_RWKV]
    wa = pm[:, 3 * D_RWKV:3 * D_RWKV + D_LORA_WA]
    gd = pm[:, 3 * D_RWKV + D_LORA_WA:D_SHIFT]
    z = w0_ref[...] + _mm(jnp.tanh(wa), wup_ref[...])
    lw_out[0] = F32(-math.exp(-0.5)) * _sigmoid(z)
    a = _sigmoid(a0_ref[...] + _mm(wa, aup_ref[...]))
    g_out[0] = _mm(_sigmoid(gd), gup_ref[...]).astype(BF16)
    ones = ones_ref[...]
    kk = k * kk_ref[...]
    kk = kk * jnp.minimum(lax.rsqrt(_head_sum(kk * kk, ones)), 1e12)
    kf = k * (1.0 + (a - 1.0) * ka_ref[...])
    r_out[0] = r
    k_out[0] = kf
    v_out[0] = v.astype(BF16)
    kkn_out[0] = kk
    b_out[0] = kk * a
    bonus_out[0] = _head_sum(r * kf * rk_ref[...], ones) * v


def _mix_pre(x, conv_state, shift_state, w, tb):
    B, T, D = x.shape
    tok = lambda width: pl.BlockSpec((1, tb, width), lambda b, t: (b, t, 0))
    per_b = lambda rows, width: pl.BlockSpec((1, rows, width), lambda b, t: (b, 0, 0))
    consts = [w["g_norm_mix"], w["w_in"], w["mu_shift"], w["w_dw"], w["b_dw"], w["ln_conv_g"], w["ln_conv_b"],
              w["w_o_conv"], w["w0"], w["w_up_pad"], w["a0"], w["a_up_pad"], w["g_up"], w["k_k"], w["k_a"], w["r_k"],
              w["ones_group"]]
    f32_tok = jax.ShapeDtypeStruct((B, T, D_RWKV), F32)
    bf_tok = jax.ShapeDtypeStruct((B, T, D_RWKV), BF16)
    out_shape = [f32_tok, f32_tok, f32_tok, bf_tok, f32_tok, f32_tok, bf_tok, f32_tok, bf_tok, bf_tok,
                 jax.ShapeDtypeStruct((B, CONV_BUF, D_CONV), F32), jax.ShapeDtypeStruct((B, 1, D_SHIFT), F32)]
    out_specs = [tok(D_RWKV)] * 10 + [per_b(CONV_BUF, D_CONV), per_b(1, D_SHIFT)]
    return pl.pallas_call(
        functools.partial(_mix_pre_kernel, tb=tb),
        grid=(B, T // tb),
        in_specs=[tok(D), per_b(CONV_PAD, D_CONV), per_b(1, D_SHIFT)] + [_const_spec(c.shape) for c in consts],
        out_specs=out_specs,
        out_shape=out_shape,
        scratch_shapes=[pltpu.VMEM((CONV_PAD + tb, D_CONV), F32), pltpu.VMEM((7, tb + CONV_PAD - 8, D_CONV), F32),
                        pltpu.VMEM((tb, D_CONV), F32),
                        pltpu.VMEM((8 + tb, D_SHIFT), F32)],
        compiler_params=pltpu.CompilerParams(dimension_semantics=("arbitrary", "arbitrary"),
                                             vmem_limit_bytes=VMEM_LIMIT),
        name="mix_pre",
    )(x, conv_state, shift_state, *consts)


def _wkv_kernel(*refs, n_chunks, has_state):
    if has_state:
        r_ref, lw_ref, k_ref, v_ref, kk_ref, b_ref, z0_ref, y_ref, zout_ref, z_scr = refs
    else:
        r_ref, lw_ref, k_ref, v_ref, kk_ref, b_ref, y_ref, zout_ref, z_scr = refs
    C, G = WKV_CHUNK, LANE_GROUP
    groups = range(N_GROUPS)

    @pl.when(pl.program_id(1) == 0)
    def _():
        z_scr[...] = z0_ref[0] if has_state else jnp.zeros((N_GROUPS, G, G), F32)

    row = lax.broadcasted_iota(jnp.int32, (G, G), 0)
    col = lax.broadcasted_iota(jnp.int32, (G, G), 1)
    same_head = (row // C) == (col // RWKV_HEAD)
    eye = (row == col).astype(F32)
    trow = lax.broadcasted_iota(jnp.int32, (C, G), 0)
    tcol = lax.broadcasted_iota(jnp.int32, (C, G), 1) % C
    strict = tcol < trow
    incl = tcol <= trow
    eye_w = (tcol == trow).astype(F32)
    tri = (lax.broadcasted_iota(jnp.int32, (C, 3 * C), 1) % C
           <= lax.broadcasted_iota(jnp.int32, (C, 3 * C), 0)).astype(BF16)

    def stack(x):
        xb = x.astype(BF16)
        return jnp.where(same_head, jnp.concatenate([xb] * HEADS_PER_GROUP, axis=0), jnp.zeros((), BF16))

    def cumsum(x):
        hi = x.astype(BF16)
        mid = (x - hi.astype(F32)).astype(BF16)
        lo = (x - hi.astype(F32) - mid.astype(F32)).astype(BF16)
        return jnp.dot(tri, jnp.concatenate([hi, mid, lo], axis=0), preferred_element_type=F32)

    n_sub = 2 if n_chunks % 2 == 0 else 1

    def chunk(ci, carry):
        rows = [pl.ds(pl.multiple_of((ci * n_sub + s) * C, C), C) for s in range(n_sub)]
        load = lambda ref: [ref[0, rows[s], g * G:(g + 1) * G] for s in range(n_sub) for g in groups]
        r, lw, k, v, kk, b = (load(ref) for ref in (r_ref, lw_ref, k_ref, v_ref, kk_ref, b_ref))
        L = [cumsum(x) for x in lw]
        Lc = [x[C - 1:C, :] for x in L]
        e_neg = [jnp.exp(-x) for x in L]
        e_end = [jnp.exp(c - x) for c, x in zip(Lc, L)]
        rt = [x * jnp.exp(l) for x, l in zip(r, L)]
        kkt = [x * jnp.exp(l - w) for x, l, w in zip(kk, L, lw)]
        q2 = [jnp.concatenate([a, c], axis=0).astype(BF16) for a, c in zip(kkt, rt)]
        a_k = [_mm_nt(q, stack(x * e)) for q, x, e in zip(q2, k, e_neg)]
        a_b = [_mm_nt(q, stack(x * e)) for q, x, e in zip(q2, b, e_neg)]
        l_k = [jnp.where(strict, a[:C], 0.0) for a in a_k]
        l_b = [jnp.where(strict, a[:C], 0.0) for a in a_b]
        a_rk = [jnp.where(incl, a[C:], 0.0) for a in a_k]
        a_rb = [jnp.where(incl, a[C:], 0.0) for a in a_b]
        xinv = [eye_w - a for a in l_b]
        p = [_mm(a, stack(a)) for a in l_b]
        n_sq = C.bit_length() - 2
        for j in range(n_sq):
            bd = [stack(a) for a in p]
            if j + 1 < n_sq:
                res = [_mm(jnp.concatenate([x, a], axis=0), m) for x, a, m in zip(xinv, p, bd)]
                xinv = [x + a[:C] for x, a in zip(xinv, res)]
                p = [a[C:] for a in res]
            else:
                xinv = [x + _mm(x, m) for x, m in zip(xinv, bd)]
        bdv = [stack(x) for x in v]
        lkv = [_mm(a, m) for a, m in zip(l_k, bdv)]
        ku = [_mm(x, jnp.concatenate([stack(a), stack(c)], axis=1)) for x, a, c in zip(xinv, kkt, lkv)]
        ru = [_mm(a, jnp.concatenate([stack(x[:, :G]), stack(x[:, G:])], axis=1)) for a, x in zip(a_rb, ku)]
        yv = [_mm(a, m) - x[:, G:] for a, m, x in zip(a_rk, bdv, ru)]
        lhs1 = [jnp.concatenate([x[:, :G], c - a[:, :G]], axis=0).astype(BF16) for x, c, a in zip(ku, rt, ru)]
        lhs2 = [jnp.concatenate([x * e, c * e, eye * jnp.exp(lc)], axis=0).astype(BF16)
                for x, c, e, lc in zip(k, b, e_end, Lc)]
        z = [z_scr[g] for g in groups]
        for s in range(n_sub):
            sl = slice(s * N_GROUPS, (s + 1) * N_GROUPS)
            res = [_mm(a, x) for a, x in zip(lhs1[sl], z)]
            u = [-(a[:C] + x[:, G:]) for a, x in zip(res, ku[sl])]
            for g in groups:
                y_ref[0, rows[s], g * G:(g + 1) * G] = res[g][C:] + yv[sl][g]
            znew = [_mm_tn(a, jnp.concatenate([c.astype(F32), x, zz], axis=0))
                    for a, c, x, zz in zip(lhs2[sl], v[sl], u, z)]
            z = [jnp.where(same_head, a, 0.0) for a in znew]
        for g in groups:
            z_scr[g] = z[g]
        return carry

    if n_chunks == n_sub:
        chunk(0, 0)
    else:
        lax.fori_loop(0, n_chunks // n_sub, chunk, 0)
    zout_ref[0] = z_scr[...]


def _wkv(r, lw, k, v, kk, b, z0, tbw):
    B, T, _ = r.shape
    has_state = z0 is not None
    tok = pl.BlockSpec((1, tbw, D_RWKV), lambda bi, t: (bi, t, 0))
    zspec = pl.BlockSpec((1, N_GROUPS, LANE_GROUP, LANE_GROUP), lambda bi, t: (bi, 0, 0, 0))
    ins = [r, lw, k, v, kk, b] + ([z0] if has_state else [])
    return pl.pallas_call(
        functools.partial(_wkv_kernel, n_chunks=tbw // WKV_CHUNK, has_state=has_state),
        grid=(B, T // tbw),
        in_specs=[tok] * 6 + ([zspec] if has_state else []),
        out_specs=[tok, zspec],
        out_shape=[jax.ShapeDtypeStruct((B, T, D_RWKV), F32),
                   jax.ShapeDtypeStruct((B, N_GROUPS, LANE_GROUP, LANE_GROUP), F32)],
        scratch_shapes=[pltpu.VMEM((N_GROUPS, LANE_GROUP, LANE_GROUP), F32)],
        compiler_params=pltpu.CompilerParams(dimension_semantics=("arbitrary", "arbitrary"),
                                             vmem_limit_bytes=VMEM_LIMIT),
        name="wkv",
    )(*ins)


def _mix_post_kernel(x_ref, y_ref, bonus_ref, g_ref, gc_ref, g2_ref, mk_ref, mv_ref, lnxg_ref, lnxb_ref, ones_ref,
                     worw_ref, wout_ref, gnx_ref, wq_ref, wox_ref, h_ref):
    ones = ones_ref[...]
    y = y_ref[0]
    inv_n = F32(1.0 / RWKV_HEAD)
    yc = y - _head_sum(y, ones) * inv_n
    var = _head_sum(yc * yc, ones) * inv_n
    yn = yc * lax.rsqrt(var + EPS_GN) * lnxg_ref[...] + lnxb_ref[...] + bonus_ref[0]
    rwkv_out = _mm(yn * g_ref[0].astype(F32), worw_ref[...])
    merged = gc_ref[0].astype(F32) + g2_ref[0].astype(F32) * rwkv_out
    h = x_ref[0] + _mm(merged, wout_ref[...])

    q = _mm(_rms(h, gnx_ref[...]), wq_ref[...])
    scale = F32(XHEAD_DIM ** -0.5)
    cols = [slice(hd * XHEAD_DIM, (hd + 1) * XHEAD_DIM) for hd in range(N_XHEADS)]
    s = [_mm_nt(q[:, cs], mk_ref[0, :, cs]) * scale for cs in cols]
    e = [jnp.exp(a - jnp.max(a, -1, keepdims=True)) for a in s]
    pw = [a * (1.0 / jnp.sum(a, -1, keepdims=True)) for a in e]
    o = [_mm(a, mv_ref[0, :, cs]).astype(BF16) for a, cs in zip(pw, cols)]
    h_ref[0] = h + jnp.dot(jnp.concatenate(o, axis=1), wox_ref[...], preferred_element_type=F32)


def _mix_post(x, y, bonus, g, gc, g2, mk, mv, w, tb):
    B, T, D = x.shape
    tok = pl.BlockSpec((1, tb, D), lambda b, t: (b, t, 0))
    mem = pl.BlockSpec((1, N_MEM, D), lambda b, t: (b, 0, 0))
    consts = [w["ln_x_g"], w["ln_x_b"], w["ones_group"], w["w_o_rwkv"], w["w_out"], w["g_norm_x"], w["w_q_x"],
              w["w_o_x"]]
    return pl.pallas_call(
        _mix_post_kernel,
        grid=(B, T // tb),
        in_specs=[tok] * 6 + [mem, mem] + [_const_spec(c.shape) for c in consts],
        out_specs=tok,
        out_shape=jax.ShapeDtypeStruct((B, T, D), F32),
        compiler_params=pltpu.CompilerParams(dimension_semantics=("arbitrary", "arbitrary"),
                                             vmem_limit_bytes=VMEM_LIMIT),
        name="mix_post",
    )(x, y, bonus, g, gc, g2, mk, mv, *consts)


def _ffn_kernel(h_ref, gffn_ref, wg_ref, wu_ref, wd_ref, gfin_ref, y_ref):
    h = h_ref[...]
    xn = _rms(h, gffn_ref[...]).astype(BF16)
    gate = jnp.dot(xn, wg_ref[...], preferred_element_type=F32)
    up = jnp.dot(xn, wu_ref[...], preferred_element_type=F32)
    h = h + _mm(gate * _sigmoid(gate) * up, wd_ref[...])
    y_ref[...] = _rms(h, gfin_ref[...])


def _ffn(h, w, tb):
    n_tok, D = h.shape
    tok = pl.BlockSpec((tb, D), lambda i: (i, 0))
    consts = [w["g_norm_ffn"], w["w_ffn_gate"], w["w_ffn_up"], w["w_ffn_down"], w["g_norm_final"]]
    return pl.pallas_call(
        _ffn_kernel,
        grid=(n_tok // tb,),
        in_specs=[tok] + [_const_spec(c.shape) for c in consts],
        out_specs=tok,
        out_shape=jax.ShapeDtypeStruct((n_tok, D), F32),
        compiler_params=pltpu.CompilerParams(dimension_semantics=("arbitrary",), vmem_limit_bytes=VMEM_LIMIT),
        name="ffn",
    )(h, *consts)


def _state_to_blockdiag(s):
    B = s.shape[0]
    st = jnp.swapaxes(s, -1, -2).reshape(B, N_GROUPS, HEADS_PER_GROUP, RWKV_HEAD, RWKV_HEAD)
    eye = jnp.eye(HEADS_PER_GROUP, dtype=s.dtype)
    z = st[:, :, :, :, None, :] * eye[None, None, :, None, :, None]
    return z.reshape(B, N_GROUPS, LANE_GROUP, LANE_GROUP)


def _blockdiag_to_state(z):
    B = z.shape[0]
    z5 = z.reshape(B, N_GROUPS, HEADS_PER_GROUP, RWKV_HEAD, HEADS_PER_GROUP, RWKV_HEAD)
    idx = jnp.arange(HEADS_PER_GROUP)
    blocks = z5[:, :, idx, :, idx, :]
    blocks = jnp.moveaxis(blocks, 0, 2)
    return jnp.swapaxes(blocks, -1, -2).reshape(B, RWKV_HEADS, RWKV_HEAD, RWKV_HEAD)


def _layer(x, mk, mv, conv_state, shift_state, wkv_state, w, tb, tb_ffn):
    B, T, D = x.shape
    conv_pad = jnp.pad(conv_state, ((0, 0), (CONV_PAD - CONV_BUF, 0), (0, 0)))
    (r, lw, k, v, kk, b, g, bonus, gc, g2, new_conv, new_shift) = _mix_pre(x, conv_pad, shift_state, w, tb)
    t_pad = -T % WKV_CHUNK
    ops = [r, lw, k, v, kk, b]
    if t_pad:
        ops = [jnp.pad(o, ((0, 0), (0, t_pad), (0, 0))) for o in ops]
    tbw = min(T + t_pad, 512)
    z0 = None if wkv_state is None else _state_to_blockdiag(wkv_state)
    y, z = _wkv(*ops, z0, tbw)
    if t_pad:
        y = y[:, :T]
    h = _mix_post(x, y, bonus, g, gc, g2, mk, mv, w, min(T, 2 * tb))
    out = _ffn(h.reshape(B * T, D), w, tb_ffn).reshape(B, T, D)
    return out, new_conv, new_shift, _blockdiag_to_state(z)


def kernel(x_prompt, x_sample, cache_mem_k, cache_mem_v, state_conv, state_shift, state_wkv, mem_prompt, g_norm_mix, w_in, mu_shift, w_dw, b_dw, ln_conv_g, ln_conv_b, w_o_conv, w0, w_up, a0, a_up, g_up, k_k, k_a, r_k, ln_x_g, ln_x_b, w_o_rwkv, w_out, g_norm_x, g_mem, w_q_x, w_k_mem, w_v_mem, w_o_x, g_norm_ffn, w_ffn_gate, w_ffn_up, w_ffn_down, g_norm_final):
    Bp, Tp, D = x_prompt.shape
    Bs, Ts, _ = x_sample.shape
    row = lambda a: a.reshape(1, -1)
    zeros_lora = jnp.zeros((D_LORA_WA // 2, D_RWKV), BF16)
    head_of = jnp.arange(LANE_GROUP) // RWKV_HEAD
    w = {
        "g_norm_mix": row(g_norm_mix[0]), "w_in": w_in[0].astype(BF16), "mu_shift": row(mu_shift[0]),
        "w_dw": jnp.repeat(w_dw[0], 8, axis=0), "b_dw": row(b_dw[0]), "ln_conv_g": row(ln_conv_g[0]), "ln_conv_b": row(ln_conv_b[0]),
        "w_o_conv": w_o_conv[0].astype(BF16), "w0": row(w0[0]),
        "w_up_pad": jnp.concatenate([w_up[0].astype(BF16), zeros_lora], 0), "a0": row(a0[0]),
        "a_up_pad": jnp.concatenate([zeros_lora, a_up[0].astype(BF16)], 0), "g_up": g_up[0].astype(BF16),
        "k_k": row(k_k[0]), "k_a": row(k_a[0]), "r_k": row(r_k[0]),
        "ones_group": (head_of[:, None] == head_of[None, :]).astype(BF16),
        "ln_x_g": row(ln_x_g[0]), "ln_x_b": row(ln_x_b[0]), "w_o_rwkv": w_o_rwkv[0].astype(BF16),
        "w_out": w_out[0].astype(BF16), "g_norm_x": row(g_norm_x[0]), "w_q_x": w_q_x[0].astype(BF16),
        "w_o_x": w_o_x[0].astype(BF16), "g_norm_ffn": row(g_norm_ffn[0]),
        "w_ffn_gate": w_ffn_gate[0].astype(BF16), "w_ffn_up": w_ffn_up[0].astype(BF16),
        "w_ffn_down": w_ffn_down[0].astype(BF16), "g_norm_final": row(g_norm_final),
    }
    mk, mv = _mem_kv(mem_prompt, row(g_mem[0]), w_k_mem[0].astype(BF16), w_v_mem[0].astype(BF16))
    conv0 = jnp.zeros((Bp, CONV_BUF, D_CONV), F32)
    shift0 = jnp.zeros((Bp, 1, D_SHIFT), F32)
    tb_p = min(Tp, 256)
    yp, cp, sp, wp = _layer(x_prompt, mk.astype(BF16), mv.astype(BF16), conv0, shift0, None, w, tb_p,
                            min(Bp * Tp, 512))
    mk_s = cache_mem_k[0].reshape(Bs, N_MEM, D).astype(BF16)
    mv_s = cache_mem_v[0].reshape(Bs, N_MEM, D).astype(BF16)
    ys, cs, ss, ws = _layer(x_sample, mk_s, mv_s, state_conv[0], state_shift[0], state_wkv[0], w, Ts,
                            min(Bs * Ts, 512))
    mem_shape = (1, Bp, N_MEM, N_XHEADS, XHEAD_DIM)
    return (yp, ys, mk.reshape(mem_shape), mv.reshape(mem_shape), cp[None], sp[None], wp[None],
            cs[None], ss[None], ws[None])
```

```python
import functools
import math

import jax
import jax.numpy as jnp
from jax import lax
from jax.experimental import pallas as pl
from jax.experimental.pallas import tpu as pltpu

F32 = jnp.float32
BF16 = jnp.bfloat16

D_MODEL = 1024
N_MEM = 256
N_XHEADS = 4
XHEAD_DIM = D_MODEL // N_XHEADS
D_CONV = 512
CONV_WIDTH = 31
CONV_BUF = CONV_WIDTH - 1
CONV_PAD = 32
D_RWKV = 1024
RWKV_HEAD = 64
RWKV_HEADS = D_RWKV // RWKV_HEAD
D_LORA_WA = 128
D_GATE_LORA = 128
D_FF = 2816
EPS_RMS = 1e-6
EPS_LN = 1e-5
EPS_GN = RWKV_HEAD * 1e-5
D_SHIFT = 3 * D_RWKV + D_LORA_WA + D_GATE_LORA
OFF_RWKV = 2 * D_CONV
OFF_GATE = OFF_RWKV + D_SHIFT
D_IN = OFF_GATE + 2 * D_MODEL

LANE_GROUP = 256
HEADS_PER_GROUP = LANE_GROUP // RWKV_HEAD
N_GROUPS = D_RWKV // LANE_GROUP
WKV_CHUNK = 64
VMEM_LIMIT = 56 * 1024 * 1024


def _mm(a, b):
    return jnp.dot(a.astype(BF16), b.astype(BF16), preferred_element_type=F32)


def _mm_nt(a, b):
    return lax.dot_general(a.astype(BF16), b.astype(BF16), (((1,), (1,)), ((), ())), preferred_element_type=F32)


def _mm_tn(a, b):
    return lax.dot_general(a.astype(BF16), b.astype(BF16), (((0,), (0,)), ((), ())), preferred_element_type=F32)


def _sigmoid(x):
    return 0.5 * jnp.tanh(0.5 * x) + 0.5


def _rms(x, g):
    return x * lax.rsqrt(jnp.mean(x * x, -1, keepdims=True) + EPS_RMS) * g


def _head_sum(x, ones):
    parts = [_mm(x[:, g * LANE_GROUP:(g + 1) * LANE_GROUP], ones) for g in range(x.shape[1] // LANE_GROUP)]
    return parts[0] if len(parts) == 1 else jnp.concatenate(parts, axis=1)


def _const_spec(shape):
    nd = len(shape)
    return pl.BlockSpec(shape, lambda *_: (0,) * nd, pipeline_mode=pl.Buffered(1))


def _mem_kv_kernel(mem_ref, g_ref, wk_ref, wv_ref, mk_ref, mv_ref):
    mn = _rms(mem_ref[0], g_ref[...]).astype(BF16)
    mk_ref[0] = jnp.dot(mn, wk_ref[...], preferred_element_type=F32)
    mv_ref[0] = jnp.dot(mn, wv_ref[...], preferred_element_type=F32)


def _mem_kv(mem, g_mem, wk, wv):
    B = mem.shape[0]
    blk = pl.BlockSpec((1, N_MEM, D_MODEL), lambda b: (b, 0, 0))
    return pl.pallas_call(
        _mem_kv_kernel,
        grid=(B,),
        in_specs=[blk, _const_spec((1, D_MODEL)), _const_spec((D_MODEL, D_MODEL)), _const_spec((D_MODEL, D_MODEL))],
        out_specs=[blk, blk],
        out_shape=[jax.ShapeDtypeStruct((B, N_MEM, D_MODEL), F32)] * 2,
        compiler_params=pltpu.CompilerParams(dimension_semantics=("arbitrary",), vmem_limit_bytes=VMEM_LIMIT),
        name="mem_kv",
    )(mem, g_mem, wk, wv)


def _mix_pre_kernel(x_ref, cst_ref, sst_ref, gmix_ref, win_ref, mu_ref, wdw_ref, bdw_ref, lng_ref, lnb_ref, woc_ref,
                    w0_ref, wup_ref, a0_ref, aup_ref, gup_ref, kk_ref, ka_ref, rk_ref, ones_ref,
                    r_out, lw_out, k_out, v_out, kkn_out, b_out, g_out, bonus_out, gc_out, g2_out,
                    nconv_out, nshift_out, ubuf, sbuf, cbuf, prbuf, *, tb):
    t = pl.program_id(1)

    @pl.when(t == 0)
    def _():
        ubuf[0:CONV_PAD, :] = cst_ref[0]
        prbuf[0:8, :] = jnp.broadcast_to(sst_ref[0], (8, D_SHIFT))

    x = x_ref[0]
    n = _rms(x, gmix_ref[...]).astype(BF16)

    cin = jnp.dot(n, win_ref[:, 0:OFF_RWKV], preferred_element_type=F32)
    ubuf[CONV_PAD:CONV_PAD + tb, :] = cin[:, :D_CONV] * _sigmoid(cin[:, D_CONV:])
    base = CONV_PAD - CONV_BUF
    span = tb + CONV_PAD - 8
    for s in range(1, 8):
        sbuf[s - 1] = ubuf[s:s + span, :]
    rows = min(tb, 32)
    n_blocks = tb // rows
    pr_chunk = 256
    n_pr_chunks = D_SHIFT // pr_chunk
    pr_done = 0
    for i in range(n_blocks):
        acc = jnp.broadcast_to(bdw_ref[...], (rows // 8, 8, D_CONV))
        for j in range(CONV_WIDTH):
            q, s = divmod(j + base, 8)
            off = 8 * q + i * rows
            tap = ubuf[off:off + rows, :] if s == 0 else sbuf[s - 1, off:off + rows, :]
            acc = acc + wdw_ref[8 * j:8 * j + 8, :][None] * tap.reshape(rows // 8, 8, D_CONV)
        cbuf[i * rows:(i + 1) * rows, :] = acc.reshape(rows, D_CONV)
        pr_upto = (i + 1) * n_pr_chunks // n_blocks
        for ch in range(pr_done, pr_upto):
            c0 = ch * pr_chunk
            prbuf[8:8 + tb, c0:c0 + pr_chunk] = jnp.dot(n, win_ref[:, OFF_RWKV + c0:OFF_RWKV + c0 + pr_chunk],
                                                        preferred_element_type=F32)
        pr_done = pr_upto
    c = cbuf[...]
    mu = jnp.mean(c, -1, keepdims=True)
    cc = c - mu
    var = jnp.mean(cc * cc, -1, keepdims=True)
    c = cc * lax.rsqrt(var + EPS_LN) * lng_ref[...] + lnb_ref[...]
    conv_out = _mm(c * _sigmoid(c), woc_ref[...])
    nconv_out[0] = ubuf[tb + base: tb + CONV_PAD, :]
    ubuf[0:CONV_PAD, :] = ubuf[tb: tb + CONV_PAD, :]

    n_gate_chunks = 2 * D_MODEL // pr_chunk
    pm_parts = []
    for ch in range(n_pr_chunks):
        cs = slice(ch * pr_chunk, (ch + 1) * pr_chunk)
        pr_c = prbuf[8:8 + tb, cs]
        pm_parts.append(pr_c + (prbuf[7:7 + tb, cs] - pr_c) * mu_ref[:, cs])
        if ch < n_gate_chunks:
            gs = slice(OFF_GATE + ch * pr_chunk, OFF_GATE + (ch + 1) * pr_chunk)
            gate = _sigmoid(jnp.dot(n, win_ref[:, gs], preferred_element_type=F32))
            if ch < n_gate_chunks // 2:
                gc_out[0, :, cs] = (gate * conv_out[:, cs]).astype(BF16)
            else:
                g2_out[0, :, ch * pr_chunk - D_MODEL:(ch + 1) * pr_chunk - D_MODEL] = gate.astype(BF16)
    pm = jnp.concatenate(pm_parts, axis=1)
    last = prbuf[7 + tb:8 + tb, :]
    nshift_out[0] = last
    prbuf[7:8, :] = last

    r = pm[:, 0:D_RWKV]
    k = pm[:, D_RWKV:2 * D_RWKV]
    v = pm[:, 2 * D_RWKV:3 * D_RWKV]
    wa = pm[:, 3 * D_RWKV:3 * D_RWKV + D_LORA_WA]
    gd = pm[:, 3 * D_RWKV + D_LORA_WA:D_SHIFT]
    z = w0_ref[...] + _mm(jnp.tanh(wa), wup_ref[...])
    lw_out[0] = F32(-math.exp(-0.5)) * _sigmoid(z)
    a = _sigmoid(a0_ref[...] + _mm(wa, aup_ref[...]))
    g_out[0] = _mm(_sigmoid(gd), gup_ref[...]).astype(BF16)
    ones = ones_ref[...]
    kk = k * kk_ref[...]
    kk = kk * jnp.minimum(lax.rsqrt(_head_sum(kk * kk, ones)), 1e12)
    kf = k * (1.0 + (a - 1.0) * ka_ref[...])
    r_out[0] = r
    k_out[0] = kf
    v_out[0] = v.astype(BF16)
    kkn_out[0] = kk
    b_out[0] = kk * a
    bonus_out[0] = _head_sum(r * kf * rk_ref[...], ones) * v


def _mix_pre(x, conv_state, shift_state, w, tb):
    B, T, D = x.shape
    tok = lambda width: pl.BlockSpec((1, tb, width), lambda b, t: (b, t, 0))
    per_b = lambda rows, width: pl.BlockSpec((1, rows, width), lambda b, t: (b, 0, 0))
    consts = [w["g_norm_mix"], w["w_in"], w["mu_shift"], w["w_dw"], w["b_dw"], w["ln_conv_g"], w["ln_conv_b"],
              w["w_o_conv"], w["w0"], w["w_up_pad"], w["a0"], w["a_up_pad"], w["g_up"], w["k_k"], w["k_a"], w["r_k"],
              w["ones_group"]]
    f32_tok = jax.ShapeDtypeStruct((B, T, D_RWKV), F32)
    bf_tok = jax.ShapeDtypeStruct((B, T, D_RWKV), BF16)
    out_shape = [f32_tok, f32_tok, f32_tok, bf_tok, f32_tok, f32_tok, bf_tok, f32_tok, bf_tok, bf_tok,
                 jax.ShapeDtypeStruct((B, CONV_BUF, D_CONV), F32), jax.ShapeDtypeStruct((B, 1, D_SHIFT), F32)]
    out_specs = [tok(D_RWKV)] * 10 + [per_b(CONV_BUF, D_CONV), per_b(1, D_SHIFT)]
    return pl.pallas_call(
        functools.partial(_mix_pre_kernel, tb=tb),
        grid=(B, T // tb),
        in_specs=[tok(D), per_b(CONV_PAD, D_CONV), per_b(1, D_SHIFT)] + [_const_spec(c.shape) for c in consts],
        out_specs=out_specs,
        out_shape=out_shape,
        scratch_shapes=[pltpu.VMEM((CONV_PAD + tb, D_CONV), F32), pltpu.VMEM((7, tb + CONV_PAD - 8, D_CONV), F32),
                        pltpu.VMEM((tb, D_CONV), F32),
                        pltpu.VMEM((8 + tb, D_SHIFT), F32)],
        compiler_params=pltpu.CompilerParams(dimension_semantics=("arbitrary", "arbitrary"),
                                             vmem_limit_bytes=VMEM_LIMIT),
        name="mix_pre",
    )(x, conv_state, shift_state, *consts)


def _wkv_kernel(*refs, n_chunks, has_state):
    if has_state:
        r_ref, lw_ref, k_ref, v_ref, kk_ref, b_ref, z0_ref, y_ref, zout_ref, z_scr = refs
    else:
        r_ref, lw_ref, k_ref, v_ref, kk_ref, b_ref, y_ref, zout_ref, z_scr = refs
    C, G = WKV_CHUNK, LANE_GROUP
    groups = range(N_GROUPS)

    @pl.when(pl.program_id(1) == 0)
    def _():
        z_scr[...] = z0_ref[0] if has_state else jnp.zeros((N_GROUPS, G, G), F32)

    row = lax.broadcasted_iota(jnp.int32, (G, G), 0)
    col = lax.broadcasted_iota(jnp.int32, (G, G), 1)
    same_head = (row // C) == (col // RWKV_HEAD)
    eye = (row == col).astype(F32)
    trow = lax.broadcasted_iota(jnp.int32, (C, G), 0)
    tcol = lax.broadcasted_iota(jnp.int32, (C, G), 1) % C
    strict = tcol < trow
    incl = tcol <= trow
    eye_w = (tcol == trow).astype(F32)
    tri = (lax.broadcasted_iota(jnp.int32, (C, 3 * C), 1) % C
           <= lax.broadcasted_iota(jnp.int32, (C, 3 * C), 0)).astype(BF16)

    def stack(x):
        xb = x.astype(BF16)
        return jnp.where(same_head, jnp.concatenate([xb] * HEADS_PER_GROUP, axis=0), jnp.zeros((), BF16))

    def cumsum(x):
        hi = x.astype(BF16)
        mid = (x - hi.astype(F32)).astype(BF16)
        lo = (x - hi.astype(F32) - mid.astype(F32)).astype(BF16)
        return jnp.dot(tri, jnp.concatenate([hi, mid, lo], axis=0), preferred_element_type=F32)

    n_sub = 4 if n_chunks % 4 == 0 else 1

    def chunk(ci, carry):
        rows = [pl.ds(pl.multiple_of((ci * n_sub + s) * C, C), C) for s in range(n_sub)]
        load = lambda ref: [ref[0, rows[s], g * G:(g + 1) * G] for s in range(n_sub) for g in groups]
        r, lw, k, v, kk, b = (load(ref) for ref in (r_ref, lw_ref, k_ref, v_ref, kk_ref, b_ref))
        L = [cumsum(x) for x in lw]
        Lc = [x[C - 1:C, :] for x in L]
        e_neg = [jnp.exp(-x) for x in L]
        e_end = [jnp.exp(c - x) for c, x in zip(Lc, L)]
        rt = [x * jnp.exp(l) for x, l in zip(r, L)]
        kkt = [x * jnp.exp(l - w) for x, l, w in zip(kk, L, lw)]
        q2 = [jnp.concatenate([a, c], axis=0).astype(BF16) for a, c in zip(kkt, rt)]
        a_k = [_mm_nt(q, stack(x * e)) for q, x, e in zip(q2, k, e_neg)]
        a_b = [_mm_nt(q, stack(x * e)) for q, x, e in zip(q2, b, e_neg)]
        l_k = [jnp.where(strict, a[:C], 0.0) for a in a_k]
        l_b = [jnp.where(strict, a[:C], 0.0) for a in a_b]
        a_rk = [jnp.where(incl, a[C:], 0.0) for a in a_k]
        a_rb = [jnp.where(incl, a[C:], 0.0) for a in a_b]
        xinv = [eye_w - a for a in l_b]
        p = [_mm(a, stack(a)) for a in l_b]
        n_sq = C.bit_length() - 2
        for j in range(n_sq):
            bd = [stack(a) for a in p]
            if j + 1 < n_sq:
                res = [_mm(jnp.concatenate([x, a], axis=0), m) for x, a, m in zip(xinv, p, bd)]
                xinv = [x + a[:C] for x, a in zip(xinv, res)]
                p = [a[C:] for a in res]
            else:
                xinv = [x + _mm(x, m) for x, m in zip(xinv, bd)]
        bdv = [stack(x) for x in v]
        lrv = [_mm(jnp.concatenate([a, c], axis=0), m) for a, c, m in zip(l_k, a_rk, bdv)]
        ku = [_mm(x, jnp.concatenate([stack(a), stack(c[:C])], axis=1)) for x, a, c in zip(xinv, kkt, lrv)]
        ru = [_mm(a, jnp.concatenate([stack(x[:, :G]), stack(x[:, G:])], axis=1)) for a, x in zip(a_rb, ku)]
        yv = [c[C:] - x[:, G:] for c, x in zip(lrv, ru)]
        lhs1 = [jnp.concatenate([x[:, :G], c - a[:, :G]], axis=0).astype(BF16) for x, c, a in zip(ku, rt, ru)]
        lhs2 = [jnp.concatenate([x * e, c * e, eye * jnp.exp(lc)], axis=0).astype(BF16)
                for x, c, e, lc in zip(k, b, e_end, Lc)]
        z = [z_scr[g] for g in groups]
        for s in range(n_sub):
            sl = slice(s * N_GROUPS, (s + 1) * N_GROUPS)
            res = [_mm(a, x) for a, x in zip(lhs1[sl], z)]
            u = [-(a[:C] + x[:, G:]) for a, x in zip(res, ku[sl])]
            for g in groups:
                y_ref[0, rows[s], g * G:(g + 1) * G] = res[g][C:] + yv[sl][g]
            znew = [_mm_tn(a, jnp.concatenate([c.astype(F32), x, zz], axis=0))
                    for a, c, x, zz in zip(lhs2[sl], v[sl], u, z)]
            z = [jnp.where(same_head, a, 0.0) for a in znew]
        for g in groups:
            z_scr[g] = z[g]
        return carry

    if n_chunks == n_sub:
        chunk(0, 0)
    else:
        lax.fori_loop(0, n_chunks // n_sub, chunk, 0)
    zout_ref[0] = z_scr[...]


def _wkv(r, lw, k, v, kk, b, z0, tbw):
    B, T, _ = r.shape
    has_state = z0 is not None
    tok = pl.BlockSpec((1, tbw, D_RWKV), lambda bi, t: (bi, t, 0))
    zspec = pl.BlockSpec((1, N_GROUPS, LANE_GROUP, LANE_GROUP), lambda bi, t: (bi, 0, 0, 0))
    ins = [r, lw, k, v, kk, b] + ([z0] if has_state else [])
    return pl.pallas_call(
        functools.partial(_wkv_kernel, n_chunks=tbw // WKV_CHUNK, has_state=has_state),
        grid=(B, T // tbw),
        in_specs=[tok] * 6 + ([zspec] if has_state else []),
        out_specs=[tok, zspec],
        out_shape=[jax.ShapeDtypeStruct((B, T, D_RWKV), F32),
                   jax.ShapeDtypeStruct((B, N_GROUPS, LANE_GROUP, LANE_GROUP), F32)],
        scratch_shapes=[pltpu.VMEM((N_GROUPS, LANE_GROUP, LANE_GROUP), F32)],
        compiler_params=pltpu.CompilerParams(dimension_semantics=("arbitrary", "arbitrary"),
                                             vmem_limit_bytes=VMEM_LIMIT),
        name="wkv",
    )(*ins)


def _mix_post_kernel(x_ref, y_ref, bonus_ref, g_ref, gc_ref, g2_ref, mk_ref, mv_ref, lnxg_ref, lnxb_ref, ones_ref,
                     worw_ref, wout_ref, gnx_ref, wq_ref, wox_ref, h_ref):
    ones = ones_ref[...]
    y = y_ref[0]
    inv_n = F32(1.0 / RWKV_HEAD)
    yc = y - _head_sum(y, ones) * inv_n
    var = _head_sum(yc * yc, ones) * inv_n
    yn = yc * lax.rsqrt(var + EPS_GN) * lnxg_ref[...] + lnxb_ref[...] + bonus_ref[0]
    rwkv_out = _mm(yn * g_ref[0].astype(F32), worw_ref[...])
    merged = gc_ref[0].astype(F32) + g2_ref[0].astype(F32) * rwkv_out
    h = x_ref[0] + _mm(merged, wout_ref[...])

    q = _mm(_rms(h, gnx_ref[...]), wq_ref[...])
    scale = F32(XHEAD_DIM ** -0.5)
    cols = [slice(hd * XHEAD_DIM, (hd + 1) * XHEAD_DIM) for hd in range(N_XHEADS)]
    s = [_mm_nt(q[:, cs], mk_ref[0, :, cs]) * scale for cs in cols]
    e = [jnp.exp(a - jnp.max(a, -1, keepdims=True)) for a in s]
    pw = [a * (1.0 / jnp.sum(a, -1, keepdims=True)) for a in e]
    o = [_mm(a, mv_ref[0, :, cs]).astype(BF16) for a, cs in zip(pw, cols)]
    h_ref[0] = h + jnp.dot(jnp.concatenate(o, axis=1), wox_ref[...], preferred_element_type=F32)


def _mix_post(x, y, bonus, g, gc, g2, mk, mv, w, tb):
    B, T, D = x.shape
    tok = pl.BlockSpec((1, tb, D), lambda b, t: (b, t, 0))
    mem = pl.BlockSpec((1, N_MEM, D), lambda b, t: (b, 0, 0))
    consts = [w["ln_x_g"], w["ln_x_b"], w["ones_group"], w["w_o_rwkv"], w["w_out"], w["g_norm_x"], w["w_q_x"],
              w["w_o_x"]]
    return pl.pallas_call(
        _mix_post_kernel,
        grid=(B, T // tb),
        in_specs=[tok] * 6 + [mem, mem] + [_const_spec(c.shape) for c in consts],
        out_specs=tok,
        out_shape=jax.ShapeDtypeStruct((B, T, D), F32),
        compiler_params=pltpu.CompilerParams(dimension_semantics=("arbitrary", "arbitrary"),
                                             vmem_limit_bytes=VMEM_LIMIT),
        name="mix_post",
    )(x, y, bonus, g, gc, g2, mk, mv, *consts)


def _ffn_kernel(h_ref, gffn_ref, wg_ref, wu_ref, wd_ref, gfin_ref, y_ref):
    h = h_ref[...]
    xn = _rms(h, gffn_ref[...]).astype(BF16)
    gate = jnp.dot(xn, wg_ref[...], preferred_element_type=F32)
    up = jnp.dot(xn, wu_ref[...], preferred_element_type=F32)
    h = h + _mm(gate * _sigmoid(gate) * up, wd_ref[...])
    y_ref[...] = _rms(h, gfin_ref[...])


def _ffn(h, w, tb):
    n_tok, D = h.shape
    tok = pl.BlockSpec((tb, D), lambda i: (i, 0))
    consts = [w["g_norm_ffn"], w["w_ffn_gate"], w["w_ffn_up"], w["w_ffn_down"], w["g_norm_final"]]
    return pl.pallas_call(
        _ffn_kernel,
        grid=(n_tok // tb,),
        in_specs=[tok] + [_const_spec(c.shape) for c in consts],
        out_specs=tok,
        out_shape=jax.ShapeDtypeStruct((n_tok, D), F32),
        compiler_params=pltpu.CompilerParams(dimension_semantics=("arbitrary",), vmem_limit_bytes=VMEM_LIMIT),
        name="ffn",
    )(h, *consts)


def _state_to_blockdiag(s):
    B = s.shape[0]
    st = jnp.swapaxes(s, -1, -2).reshape(B, N_GROUPS, HEADS_PER_GROUP, RWKV_HEAD, RWKV_HEAD)
    eye = jnp.eye(HEADS_PER_GROUP, dtype=s.dtype)
    z = st[:, :, :, :, None, :] * eye[None, None, :, None, :, None]
    return z.reshape(B, N_GROUPS, LANE_GROUP, LANE_GROUP)


def _blockdiag_to_state(z):
    B = z.shape[0]
    z5 = z.reshape(B, N_GROUPS, HEADS_PER_GROUP, RWKV_HEAD, HEADS_PER_GROUP, RWKV_HEAD)
    idx = jnp.arange(HEADS_PER_GROUP)
    blocks = z5[:, :, idx, :, idx, :]
    blocks = jnp.moveaxis(blocks, 0, 2)
    return jnp.swapaxes(blocks, -1, -2).reshape(B, RWKV_HEADS, RWKV_HEAD, RWKV_HEAD)


def _layer(x, mk, mv, conv_state, shift_state, wkv_state, w, tb, tb_ffn):
    B, T, D = x.shape
    conv_pad = jnp.pad(conv_state, ((0, 0), (CONV_PAD - CONV_BUF, 0), (0, 0)))
    (r, lw, k, v, kk, b, g, bonus, gc, g2, new_conv, new_shift) = _mix_pre(x, conv_pad, shift_state, w, tb)
    t_pad = -T % WKV_CHUNK
    ops = [r, lw, k, v, kk, b]
    if t_pad:
        ops = [jnp.pad(o, ((0, 0), (0, t_pad), (0, 0))) for o in ops]
    tbw = min(T + t_pad, 512)
    z0 = None if wkv_state is None else _state_to_blockdiag(wkv_state)
    y, z = _wkv(*ops, z0, tbw)
    if t_pad:
        y = y[:, :T]
    h = _mix_post(x, y, bonus, g, gc, g2, mk, mv, w, min(T, 2 * tb))
    out = _ffn(h.reshape(B * T, D), w, tb_ffn).reshape(B, T, D)
    return out, new_conv, new_shift, _blockdiag_to_state(z)


def kernel(x_prompt, x_sample, cache_mem_k, cache_mem_v, state_conv, state_shift, state_wkv, mem_prompt, g_norm_mix, w_in, mu_shift, w_dw, b_dw, ln_conv_g, ln_conv_b, w_o_conv, w0, w_up, a0, a_up, g_up, k_k, k_a, r_k, ln_x_g, ln_x_b, w_o_rwkv, w_out, g_norm_x, g_mem, w_q_x, w_k_mem, w_v_mem, w_o_x, g_norm_ffn, w_ffn_gate, w_ffn_up, w_ffn_down, g_norm_final):
    Bp, Tp, D = x_prompt.shape
    Bs, Ts, _ = x_sample.shape
    row = lambda a: a.reshape(1, -1)
    zeros_lora = jnp.zeros((D_LORA_WA // 2, D_RWKV), BF16)
    head_of = jnp.arange(LANE_GROUP) // RWKV_HEAD
    w = {
        "g_norm_mix": row(g_norm_mix[0]), "w_in": w_in[0].astype(BF16), "mu_shift": row(mu_shift[0]),
        "w_dw": jnp.repeat(w_dw[0], 8, axis=0), "b_dw": row(b_dw[0]), "ln_conv_g": row(ln_conv_g[0]), "ln_conv_b": row(ln_conv_b[0]),
        "w_o_conv": w_o_conv[0].astype(BF16), "w0": row(w0[0]),
        "w_up_pad": jnp.concatenate([w_up[0].astype(BF16), zeros_lora], 0), "a0": row(a0[0]),
        "a_up_pad": jnp.concatenate([zeros_lora, a_up[0].astype(BF16)], 0), "g_up": g_up[0].astype(BF16),
        "k_k": row(k_k[0]), "k_a": row(k_a[0]), "r_k": row(r_k[0]),
        "ones_group": (head_of[:, None] == head_of[None, :]).astype(BF16),
        "ln_x_g": row(ln_x_g[0]), "ln_x_b": row(ln_x_b[0]), "w_o_rwkv": w_o_rwkv[0].astype(BF16),
        "w_out": w_out[0].astype(BF16), "g_norm_x": row(g_norm_x[0]), "w_q_x": w_q_x[0].astype(BF16),
        "w_o_x": w_o_x[0].astype(BF16), "g_norm_ffn": row(g_norm_ffn[0]),
        "w_ffn_gate": w_ffn_gate[0].astype(BF16), "w_ffn_up": w_ffn_up[0].astype(BF16),
        "w_ffn_down": w_ffn_down[0].astype(BF16), "g_norm_final": row(g_norm_final),
    }
    mk, mv = _mem_kv(mem_prompt, row(g_mem[0]), w_k_mem[0].astype(BF16), w_v_mem[0].astype(BF16))
    conv0 = jnp.zeros((Bp, CONV_BUF, D_CONV), F32)
    shift0 = jnp.zeros((Bp, 1, D_SHIFT), F32)
    tb_p = min(Tp, 256)
    yp, cp, sp, wp = _layer(x_prompt, mk.astype(BF16), mv.astype(BF16), conv0, shift0, None, w, tb_p,
                            min(Bp * Tp, 512))
    mk_s = cache_mem_k[0].reshape(Bs, N_MEM, D).astype(BF16)
    mv_s = cache_mem_v[0].reshape(Bs, N_MEM, D).astype(BF16)
    ys, cs, ss, ws = _layer(x_sample, mk_s, mv_s, state_conv[0], state_shift[0], state_wkv[0], w, Ts,
                            min(Bs * Ts, 512))
    mem_shape = (1, Bp, N_MEM, N_XHEADS, XHEAD_DIM)
    return (yp, ys, mk.reshape(mem_shape), mv.reshape(mem_shape), cp[None], sp[None], wp[None],
            cs[None], ss[None], ws[None])
```

```python
import functools
import math

import jax
import jax.numpy as jnp
from jax import lax
from jax.experimental import pallas as pl
from jax.experimental.pallas import tpu as pltpu

F32 = jnp.float32
BF16 = jnp.bfloat16

D_MODEL = 1024
N_MEM = 256
N_XHEADS = 4
XHEAD_DIM = D_MODEL // N_XHEADS
D_CONV = 512
CONV_WIDTH = 31
CONV_BUF = CONV_WIDTH - 1
CONV_PAD = 32
D_RWKV = 1024
RWKV_HEAD = 64
RWKV_HEADS = D_RWKV // RWKV_HEAD
D_LORA_WA = 128
D_GATE_LORA = 128
D_FF = 2816
EPS_RMS = 1e-6
EPS_LN = 1e-5
EPS_GN = RWKV_HEAD * 1e-5
D_SHIFT = 3 * D_RWKV + D_LORA_WA + D_GATE_LORA
OFF_RWKV = 2 * D_CONV
OFF_GATE = OFF_RWKV + D_SHIFT
D_IN = OFF_GATE + 2 * D_MODEL

LANE_GROUP = 256
HEADS_PER_GROUP = LANE_GROUP // RWKV_HEAD
N_GROUPS = D_RWKV // LANE_GROUP
WKV_CHUNK = 64
VMEM_LIMIT = 56 * 1024 * 1024


def _mm(a, b):
    return jnp.dot(a.astype(BF16), b.astype(BF16), preferred_element_type=F32)


def _mm_nt(a, b):
    return lax.dot_general(a.astype(BF16), b.astype(BF16), (((1,), (1,)), ((), ())), preferred_element_type=F32)


def _mm_tn(a, b):
    return lax.dot_general(a.astype(BF16), b.astype(BF16), (((0,), (0,)), ((), ())), preferred_element_type=F32)


def _sigmoid(x):
    return 0.5 * jnp.tanh(0.5 * x) + 0.5


def _rms(x, g):
    return x * lax.rsqrt(jnp.mean(x * x, -1, keepdims=True) + EPS_RMS) * g


def _head_sum(x, ones):
    parts = [_mm(x[:, g * LANE_GROUP:(g + 1) * LANE_GROUP], ones) for g in range(x.shape[1] // LANE_GROUP)]
    return parts[0] if len(parts) == 1 else jnp.concatenate(parts, axis=1)


def _const_spec(shape):
    nd = len(shape)
    return pl.BlockSpec(shape, lambda *_: (0,) * nd, pipeline_mode=pl.Buffered(1))


def _mem_kv_kernel(mem_ref, g_ref, wk_ref, wv_ref, mk_ref, mv_ref):
    mn = _rms(mem_ref[0], g_ref[...]).astype(BF16)
    mk_ref[0] = jnp.dot(mn, wk_ref[...], preferred_element_type=F32)
    mv_ref[0] = jnp.dot(mn, wv_ref[...], preferred_element_type=F32)


def _mem_kv(mem, g_mem, wk, wv):
    B = mem.shape[0]
    blk = pl.BlockSpec((1, N_MEM, D_MODEL), lambda b: (b, 0, 0))
    return pl.pallas_call(
        _mem_kv_kernel,
        grid=(B,),
        in_specs=[blk, _const_spec((1, D_MODEL)), _const_spec((D_MODEL, D_MODEL)), _const_spec((D_MODEL, D_MODEL))],
        out_specs=[blk, blk],
        out_shape=[jax.ShapeDtypeStruct((B, N_MEM, D_MODEL), F32)] * 2,
        compiler_params=pltpu.CompilerParams(dimension_semantics=("arbitrary",), vmem_limit_bytes=VMEM_LIMIT),
        name="mem_kv",
    )(mem, g_mem, wk, wv)


def _mix_pre_kernel(x_ref, cst_ref, sst_ref, gmix_ref, win_ref, mu_ref, wdw_ref, bdw_ref, lng_ref, lnb_ref, woc_ref,
                    w0_ref, wup_ref, a0_ref, aup_ref, gup_ref, kk_ref, ka_ref, rk_ref, ones_ref,
                    r_out, lw_out, k_out, v_out, kkn_out, b_out, g_out, bonus_out, gc_out, g2_out,
                    nconv_out, nshift_out, ubuf, sbuf, cbuf, prbuf, *, tb):
    t = pl.program_id(1)

    @pl.when(t == 0)
    def _():
        ubuf[0:CONV_PAD, :] = cst_ref[0]
        prbuf[0:8, :] = jnp.broadcast_to(sst_ref[0], (8, D_SHIFT))

    x = x_ref[0]
    n = _rms(x, gmix_ref[...]).astype(BF16)

    cin = jnp.dot(n, win_ref[:, 0:OFF_RWKV], preferred_element_type=F32)
    ubuf[CONV_PAD:CONV_PAD + tb, :] = cin[:, :D_CONV] * _sigmoid(cin[:, D_CONV:])
    base = CONV_PAD - CONV_BUF
    span = tb + CONV_PAD - 8
    for s in range(1, 8):
        sbuf[s - 1] = ubuf[s:s + span, :]
    rows = min(tb, 32)
    n_blocks = tb // rows
    pr_chunk = 256
    n_pr_chunks = D_SHIFT // pr_chunk
    pr_done = 0
    for i in range(n_blocks):
        acc = jnp.broadcast_to(bdw_ref[...], (rows // 8, 8, D_CONV))
        for j in range(CONV_WIDTH):
            q, s = divmod(j + base, 8)
            off = 8 * q + i * rows
            tap = ubuf[off:off + rows, :] if s == 0 else sbuf[s - 1, off:off + rows, :]
            acc = acc + wdw_ref[8 * j:8 * j + 8, :][None] * tap.reshape(rows // 8, 8, D_CONV)
        cbuf[i * rows:(i + 1) * rows, :] = acc.reshape(rows, D_CONV)
        pr_upto = (i + 1) * n_pr_chunks // n_blocks
        for ch in range(pr_done, pr_upto):
            c0 = ch * pr_chunk
            prbuf[8:8 + tb, c0:c0 + pr_chunk] = jnp.dot(n, win_ref[:, OFF_RWKV + c0:OFF_RWKV + c0 + pr_chunk],
                                                        preferred_element_type=F32)
        pr_done = pr_upto
    c = cbuf[...]
    mu = jnp.mean(c, -1, keepdims=True)
    cc = c - mu
    var = jnp.mean(cc * cc, -1, keepdims=True)
    c = cc * lax.rsqrt(var + EPS_LN) * lng_ref[...] + lnb_ref[...]
    conv_out = _mm(c * _sigmoid(c), woc_ref[...])
    nconv_out[0] = ubuf[tb + base: tb + CONV_PAD, :]
    ubuf[0:CONV_PAD, :] = ubuf[tb: tb + CONV_PAD, :]

    n_gate_chunks = 2 * D_MODEL // pr_chunk
    pm_parts = []
    for ch in range(n_pr_chunks):
        cs = slice(ch * pr_chunk, (ch + 1) * pr_chunk)
        pr_c = prbuf[8:8 + tb, cs]
        pm_parts.append(pr_c + (prbuf[7:7 + tb, cs] - pr_c) * mu_ref[:, cs])
        if ch < n_gate_chunks:
            gs = slice(OFF_GATE + ch * pr_chunk, OFF_GATE + (ch + 1) * pr_chunk)
            gate = _sigmoid(jnp.dot(n, win_ref[:, gs], preferred_element_type=F32))
            if ch < n_gate_chunks // 2:
                gc_out[0, :, cs] = (gate * conv_out[:, cs]).astype(BF16)
            else:
                g2_out[0, :, ch * pr_chunk - D_MODEL:(ch + 1) * pr_chunk - D_MODEL] = gate.astype(BF16)
    pm = jnp.concatenate(pm_parts, axis=1)
    last = prbuf[7 + tb:8 + tb, :]
    nshift_out[0] = last
    prbuf[7:8, :] = last

    r = pm[:, 0:D_RWKV]
    k = pm[:, D_RWKV:2 * D_RWKV]
    v = pm[:, 2 * D_RWKV:3 * D_RWKV]
    wa = pm[:, 3 * D_RWKV:3 * D_RWKV + D_LORA_WA]
    gd = pm[:, 3 * D_RWKV + D_LORA_WA:D_SHIFT]
    z = w0_ref[...] + _mm(jnp.tanh(wa), wup_ref[...])
    lw_out[0] = F32(-math.exp(-0.5)) * _sigmoid(z)
    a = _sigmoid(a0_ref[...] + _mm(wa, aup_ref[...]))
    g_out[0] = _mm(_sigmoid(gd), gup_ref[...]).astype(BF16)
    ones = ones_ref[...]
    kk = k * kk_ref[...]
    kk = kk * jnp.minimum(lax.rsqrt(_head_sum(kk * kk, ones)), 1e12)
    kf = k * (1.0 + (a - 1.0) * ka_ref[...])
    r_out[0] = r
    k_out[0] = kf
    v_out[0] = v.astype(BF16)
    kkn_out[0] = kk
    b_out[0] = kk * a
    bonus_out[0] = _head_sum(r * kf * rk_ref[...], ones) * v


def _mix_pre(x, conv_state, shift_state, w, tb):
    B, T, D = x.shape
    tok = lambda width: pl.BlockSpec((1, tb, width), lambda b, t: (b, t, 0))
    per_b = lambda rows, width: pl.BlockSpec((1, rows, width), lambda b, t: (b, 0, 0))
    consts = [w["g_norm_mix"], w["w_in"], w["mu_shift"], w["w_dw"], w["b_dw"], w["ln_conv_g"], w["ln_conv_b"],
              w["w_o_conv"], w["w0"], w["w_up_pad"], w["a0"], w["a_up_pad"], w["g_up"], w["k_k"], w["k_a"], w["r_k"],
              w["ones_group"]]
    f32_tok = jax.ShapeDtypeStruct((B, T, D_RWKV), F32)
    bf_tok = jax.ShapeDtypeStruct((B, T, D_RWKV), BF16)
    out_shape = [f32_tok, f32_tok, f32_tok, bf_tok, f32_tok, f32_tok, bf_tok, f32_tok, bf_tok, bf_tok,
                 jax.ShapeDtypeStruct((B, CONV_BUF, D_CONV), F32), jax.ShapeDtypeStruct((B, 1, D_SHIFT), F32)]
    out_specs = [tok(D_RWKV)] * 10 + [per_b(CONV_BUF, D_CONV), per_b(1, D_SHIFT)]
    return pl.pallas_call(
        functools.partial(_mix_pre_kernel, tb=tb),
        grid=(B, T // tb),
        in_specs=[tok(D), per_b(CONV_PAD, D_CONV), per_b(1, D_SHIFT)] + [_const_spec(c.shape) for c in consts],
        out_specs=out_specs,
        out_shape=out_shape,
        scratch_shapes=[pltpu.VMEM((CONV_PAD + tb, D_CONV), F32), pltpu.VMEM((7, tb + CONV_PAD - 8, D_CONV), F32),
                        pltpu.VMEM((tb, D_CONV), F32),
                        pltpu.VMEM((8 + tb, D_SHIFT), F32)],
        compiler_params=pltpu.CompilerParams(dimension_semantics=("arbitrary", "arbitrary"),
                                             vmem_limit_bytes=VMEM_LIMIT),
        name="mix_pre",
    )(x, conv_state, shift_state, *consts)


def _wkv_kernel(*refs, n_chunks, has_state):
    if has_state:
        r_ref, lw_ref, k_ref, v_ref, kk_ref, b_ref, z0_ref, y_ref, zout_ref, z_scr = refs
    else:
        r_ref, lw_ref, k_ref, v_ref, kk_ref, b_ref, y_ref, zout_ref, z_scr = refs
    C, G = WKV_CHUNK, LANE_GROUP
    groups = range(N_GROUPS)

    @pl.when(pl.program_id(1) == 0)
    def _():
        z_scr[...] = z0_ref[0] if has_state else jnp.zeros((N_GROUPS, G, G), F32)

    row = lax.broadcasted_iota(jnp.int32, (G, G), 0)
    col = lax.broadcasted_iota(jnp.int32, (G, G), 1)
    same_head = (row // C) == (col // RWKV_HEAD)
    eye = (row == col).astype(F32)
    trow = lax.broadcasted_iota(jnp.int32, (C, G), 0)
    tcol = lax.broadcasted_iota(jnp.int32, (C, G), 1) % C
    strict = tcol < trow
    incl = tcol <= trow
    eye_w = (tcol == trow).astype(F32)
    tri = (lax.broadcasted_iota(jnp.int32, (C, 3 * C), 1) % C
           <= lax.broadcasted_iota(jnp.int32, (C, 3 * C), 0)).astype(BF16)

    def stack(x):
        xb = x.astype(BF16)
        return jnp.where(same_head, jnp.concatenate([xb] * HEADS_PER_GROUP, axis=0), jnp.zeros((), BF16))

    def cumsum(x):
        hi = x.astype(BF16)
        mid = (x - hi.astype(F32)).astype(BF16)
        lo = (x - hi.astype(F32) - mid.astype(F32)).astype(BF16)
        return jnp.dot(tri, jnp.concatenate([hi, mid, lo], axis=0), preferred_element_type=F32)

    n_sub = 8 if n_chunks % 8 == 0 else 1

    def chunk(ci, carry):
        rows = [pl.ds(pl.multiple_of((ci * n_sub + s) * C, C), C) for s in range(n_sub)]
        load = lambda ref: [ref[0, rows[s], g * G:(g + 1) * G] for s in range(n_sub) for g in groups]
        r, lw, k, v, kk, b = (load(ref) for ref in (r_ref, lw_ref, k_ref, v_ref, kk_ref, b_ref))
        L = [cumsum(x) for x in lw]
        Lc = [x[C - 1:C, :] for x in L]
        e_neg = [jnp.exp(-x) for x in L]
        e_end = [jnp.exp(c - x) for c, x in zip(Lc, L)]
        rt = [x * jnp.exp(l) for x, l in zip(r, L)]
        kkt = [x * jnp.exp(l - w) for x, l, w in zip(kk, L, lw)]
        q2 = [jnp.concatenate([a, c], axis=0).astype(BF16) for a, c in zip(kkt, rt)]
        a_k = [_mm_nt(q, stack(x * e)) for q, x, e in zip(q2, k, e_neg)]
        a_b = [_mm_nt(q, stack(x * e)) for q, x, e in zip(q2, b, e_neg)]
        l_k = [jnp.where(strict, a[:C], 0.0) for a in a_k]
        l_b = [jnp.where(strict, a[:C], 0.0) for a in a_b]
        a_rk = [jnp.where(incl, a[C:], 0.0) for a in a_k]
        a_rb = [jnp.where(incl, a[C:], 0.0) for a in a_b]
        xinv = [eye_w - a for a in l_b]
        p = [_mm(a, stack(a)) for a in l_b]
        n_sq = C.bit_length() - 2
        for j in range(n_sq):
            bd = [stack(a) for a in p]
            if j + 1 < n_sq:
                res = [_mm(jnp.concatenate([x, a], axis=0), m) for x, a, m in zip(xinv, p, bd)]
                xinv = [x + a[:C] for x, a in zip(xinv, res)]
                p = [a[C:] for a in res]
            else:
                xinv = [x + _mm(x, m) for x, m in zip(xinv, bd)]
        bdv = [stack(x) for x in v]
        lrv = [_mm(jnp.concatenate([a, c], axis=0), m) for a, c, m in zip(l_k, a_rk, bdv)]
        ku = [_mm(x, jnp.concatenate([stack(a), stack(c[:C])], axis=1)) for x, a, c in zip(xinv, kkt, lrv)]
        ru = [_mm(a, jnp.concatenate([stack(x[:, :G]), stack(x[:, G:])], axis=1)) for a, x in zip(a_rb, ku)]
        yv = [c[C:] - x[:, G:] for c, x in zip(lrv, ru)]
        lhs1 = [jnp.concatenate([x[:, :G], c - a[:, :G]], axis=0).astype(BF16) for x, c, a in zip(ku, rt, ru)]
        lhs2 = [jnp.concatenate([x * e, c * e, eye * jnp.exp(lc)], axis=0).astype(BF16)
                for x, c, e, lc in zip(k, b, e_end, Lc)]
        z = [z_scr[g] for g in groups]
        for s in range(n_sub):
            sl = slice(s * N_GROUPS, (s + 1) * N_GROUPS)
            res = [_mm(a, x) for a, x in zip(lhs1[sl], z)]
            u = [-(a[:C] + x[:, G:]) for a, x in zip(res, ku[sl])]
            for g in groups:
                y_ref[0, rows[s], g * G:(g + 1) * G] = res[g][C:] + yv[sl][g]
            znew = [_mm_tn(a, jnp.concatenate([c.astype(F32), x, zz], axis=0))
                    for a, c, x, zz in zip(lhs2[sl], v[sl], u, z)]
            z = [jnp.where(same_head, a, 0.0) for a in znew]
        for g in groups:
            z_scr[g] = z[g]
        return carry

    if n_chunks == n_sub:
        chunk(0, 0)
    else:
        lax.fori_loop(0, n_chunks // n_sub, chunk, 0)
    zout_ref[0] = z_scr[...]


def _wkv(r, lw, k, v, kk, b, z0, tbw):
    B, T, _ = r.shape
    has_state = z0 is not None
    tok = pl.BlockSpec((1, tbw, D_RWKV), lambda bi, t: (bi, t, 0))
    zspec = pl.BlockSpec((1, N_GROUPS, LANE_GROUP, LANE_GROUP), lambda bi, t: (bi, 0, 0, 0))
    ins = [r, lw, k, v, kk, b] + ([z0] if has_state else [])
    return pl.pallas_call(
        functools.partial(_wkv_kernel, n_chunks=tbw // WKV_CHUNK, has_state=has_state),
        grid=(B, T // tbw),
        in_specs=[tok] * 6 + ([zspec] if has_state else []),
        out_specs=[tok, zspec],
        out_shape=[jax.ShapeDtypeStruct((B, T, D_RWKV), F32),
                   jax.ShapeDtypeStruct((B, N_GROUPS, LANE_GROUP, LANE_GROUP), F32)],
        scratch_shapes=[pltpu.VMEM((N_GROUPS, LANE_GROUP, LANE_GROUP), F32)],
        compiler_params=pltpu.CompilerParams(dimension_semantics=("arbitrary", "arbitrary"),
                                             vmem_limit_bytes=VMEM_LIMIT),
        name="wkv",
    )(*ins)


def _mix_post_kernel(x_ref, y_ref, bonus_ref, g_ref, gc_ref, g2_ref, mk_ref, mv_ref, lnxg_ref, lnxb_ref, ones_ref,
                     worw_ref, wout_ref, gnx_ref, wq_ref, wox_ref, h_ref):
    ones = ones_ref[...]
    y = y_ref[0]
    inv_n = F32(1.0 / RWKV_HEAD)
    yc = y - _head_sum(y, ones) * inv_n
    var = _head_sum(yc * yc, ones) * inv_n
    yn = yc * lax.rsqrt(var + EPS_GN) * lnxg_ref[...] + lnxb_ref[...] + bonus_ref[0]
    rwkv_out = _mm(yn * g_ref[0].astype(F32), worw_ref[...])
    merged = gc_ref[0].astype(F32) + g2_ref[0].astype(F32) * rwkv_out
    h = x_ref[0] + _mm(merged, wout_ref[...])

    q = _mm(_rms(h, gnx_ref[...]), wq_ref[...])
    scale = F32(XHEAD_DIM ** -0.5)
    cols = [slice(hd * XHEAD_DIM, (hd + 1) * XHEAD_DIM) for hd in range(N_XHEADS)]
    s = [_mm_nt(q[:, cs], mk_ref[0, :, cs]) * scale for cs in cols]
    e = [jnp.exp(a - jnp.max(a, -1, keepdims=True)) for a in s]
    pw = [a * (1.0 / jnp.sum(a, -1, keepdims=True)) for a in e]
    o = [_mm(a, mv_ref[0, :, cs]).astype(BF16) for a, cs in zip(pw, cols)]
    h_ref[0] = h + jnp.dot(jnp.concatenate(o, axis=1), wox_ref[...], preferred_element_type=F32)


def _mix_post(x, y, bonus, g, gc, g2, mk, mv, w, tb):
    B, T, D = x.shape
    tok = pl.BlockSpec((1, tb, D), lambda b, t: (b, t, 0))
    mem = pl.BlockSpec((1, N_MEM, D), lambda b, t: (b, 0, 0))
    consts = [w["ln_x_g"], w["ln_x_b"], w["ones_group"], w["w_o_rwkv"], w["w_out"], w["g_norm_x"], w["w_q_x"],
              w["w_o_x"]]
    return pl.pallas_call(
        _mix_post_kernel,
        grid=(B, T // tb),
        in_specs=[tok] * 6 + [mem, mem] + [_const_spec(c.shape) for c in consts],
        out_specs=tok,
        out_shape=jax.ShapeDtypeStruct((B, T, D), F32),
        compiler_params=pltpu.CompilerParams(dimension_semantics=("arbitrary", "arbitrary"),
                                             vmem_limit_bytes=VMEM_LIMIT),
        name="mix_post",
    )(x, y, bonus, g, gc, g2, mk, mv, *consts)


def _ffn_kernel(h_ref, gffn_ref, wg_ref, wu_ref, wd_ref, gfin_ref, y_ref):
    h = h_ref[...]
    xn = _rms(h, gffn_ref[...]).astype(BF16)
    gate = jnp.dot(xn, wg_ref[...], preferred_element_type=F32)
    up = jnp.dot(xn, wu_ref[...], preferred_element_type=F32)
    h = h + _mm(gate * _sigmoid(gate) * up, wd_ref[...])
    y_ref[...] = _rms(h, gfin_ref[...])


def _ffn(h, w, tb):
    n_tok, D = h.shape
    tok = pl.BlockSpec((tb, D), lambda i: (i, 0))
    consts = [w["g_norm_ffn"], w["w_ffn_gate"], w["w_ffn_up"], w["w_ffn_down"], w["g_norm_final"]]
    return pl.pallas_call(
        _ffn_kernel,
        grid=(n_tok // tb,),
        in_specs=[tok] + [_const_spec(c.shape) for c in consts],
        out_specs=tok,
        out_shape=jax.ShapeDtypeStruct((n_tok, D), F32),
        compiler_params=pltpu.CompilerParams(dimension_semantics=("arbitrary",), vmem_limit_bytes=VMEM_LIMIT),
        name="ffn",
    )(h, *consts)


def _state_to_blockdiag(s):
    B = s.shape[0]
    st = jnp.swapaxes(s, -1, -2).reshape(B, N_GROUPS, HEADS_PER_GROUP, RWKV_HEAD, RWKV_HEAD)
    eye = jnp.eye(HEADS_PER_GROUP, dtype=s.dtype)
    z = st[:, :, :, :, None, :] * eye[None, None, :, None, :, None]
    return z.reshape(B, N_GROUPS, LANE_GROUP, LANE_GROUP)


def _blockdiag_to_state(z):
    B = z.shape[0]
    z5 = z.reshape(B, N_GROUPS, HEADS_PER_GROUP, RWKV_HEAD, HEADS_PER_GROUP, RWKV_HEAD)
    idx = jnp.arange(HEADS_PER_GROUP)
    blocks = z5[:, :, idx, :, idx, :]
    blocks = jnp.moveaxis(blocks, 0, 2)
    return jnp.swapaxes(blocks, -1, -2).reshape(B, RWKV_HEADS, RWKV_HEAD, RWKV_HEAD)


def _layer(x, mk, mv, conv_state, shift_state, wkv_state, w, tb, tb_ffn):
    B, T, D = x.shape
    conv_pad = jnp.pad(conv_state, ((0, 0), (CONV_PAD - CONV_BUF, 0), (0, 0)))
    (r, lw, k, v, kk, b, g, bonus, gc, g2, new_conv, new_shift) = _mix_pre(x, conv_pad, shift_state, w, tb)
    t_pad = -T % WKV_CHUNK
    ops = [r, lw, k, v, kk, b]
    if t_pad:
        ops = [jnp.pad(o, ((0, 0), (0, t_pad), (0, 0))) for o in ops]
    tbw = min(T + t_pad, 512)
    z0 = None if wkv_state is None else _state_to_blockdiag(wkv_state)
    y, z = _wkv(*ops, z0, tbw)
    if t_pad:
        y = y[:, :T]
    h = _mix_post(x, y, bonus, g, gc, g2, mk, mv, w, min(T, 2 * tb))
    out = _ffn(h.reshape(B * T, D), w, tb_ffn).reshape(B, T, D)
    return out, new_conv, new_shift, _blockdiag_to_state(z)


def kernel(x_prompt, x_sample, cache_mem_k, cache_mem_v, state_conv, state_shift, state_wkv, mem_prompt, g_norm_mix, w_in, mu_shift, w_dw, b_dw, ln_conv_g, ln_conv_b, w_o_conv, w0, w_up, a0, a_up, g_up, k_k, k_a, r_k, ln_x_g, ln_x_b, w_o_rwkv, w_out, g_norm_x, g_mem, w_q_x, w_k_mem, w_v_mem, w_o_x, g_norm_ffn, w_ffn_gate, w_ffn_up, w_ffn_down, g_norm_final):
    Bp, Tp, D = x_prompt.shape
    Bs, Ts, _ = x_sample.shape
    row = lambda a: a.reshape(1, -1)
    zeros_lora = jnp.zeros((D_LORA_WA // 2, D_RWKV), BF16)
    head_of = jnp.arange(LANE_GROUP) // RWKV_HEAD
    w = {
        "g_norm_mix": row(g_norm_mix[0]), "w_in": w_in[0].astype(BF16), "mu_shift": row(mu_shift[0]),
        "w_dw": jnp.repeat(w_dw[0], 8, axis=0), "b_dw": row(b_dw[0]), "ln_conv_g": row(ln_conv_g[0]), "ln_conv_b": row(ln_conv_b[0]),
        "w_o_conv": w_o_conv[0].astype(BF16), "w0": row(w0[0]),
        "w_up_pad": jnp.concatenate([w_up[0].astype(BF16), zeros_lora], 0), "a0": row(a0[0]),
        "a_up_pad": jnp.concatenate([zeros_lora, a_up[0].astype(BF16)], 0), "g_up": g_up[0].astype(BF16),
        "k_k": row(k_k[0]), "k_a": row(k_a[0]), "r_k": row(r_k[0]),
        "ones_group": (head_of[:, None] == head_of[None, :]).astype(BF16),
        "ln_x_g": row(ln_x_g[0]), "ln_x_b": row(ln_x_b[0]), "w_o_rwkv": w_o_rwkv[0].astype(BF16),
        "w_out": w_out[0].astype(BF16), "g_norm_x": row(g_norm_x[0]), "w_q_x": w_q_x[0].astype(BF16),
        "w_o_x": w_o_x[0].astype(BF16), "g_norm_ffn": row(g_norm_ffn[0]),
        "w_ffn_gate": w_ffn_gate[0].astype(BF16), "w_ffn_up": w_ffn_up[0].astype(BF16),
        "w_ffn_down": w_ffn_down[0].astype(BF16), "g_norm_final": row(g_norm_final),
    }
    mk, mv = _mem_kv(mem_prompt, row(g_mem[0]), w_k_mem[0].astype(BF16), w_v_mem[0].astype(BF16))
    conv0 = jnp.zeros((Bp, CONV_BUF, D_CONV), F32)
    shift0 = jnp.zeros((Bp, 1, D_SHIFT), F32)
    tb_p = min(Tp, 256)
    yp, cp, sp, wp = _layer(x_prompt, mk.astype(BF16), mv.astype(BF16), conv0, shift0, None, w, tb_p,
                            min(Bp * Tp, 512))
    mk_s = cache_mem_k[0].reshape(Bs, N_MEM, D).astype(BF16)
    mv_s = cache_mem_v[0].reshape(Bs, N_MEM, D).astype(BF16)
    ys, cs, ss, ws = _layer(x_sample, mk_s, mv_s, state_conv[0], state_shift[0], state_wkv[0], w, Ts,
                            min(Bs * Ts, 512))
    mem_shape = (1, Bp, N_MEM, N_XHEADS, XHEAD_DIM)
    return (yp, ys, mk.reshape(mem_shape), mv.reshape(mem_shape), cp[None], sp[None], wp[None],
            cs[None], ss[None], ws[None])
```

```python
import functools
import math

import jax
import jax.numpy as jnp
from jax import lax
from jax.experimental import pallas as pl
from jax.experimental.pallas import tpu as pltpu

F32 = jnp.float32
BF16 = jnp.bfloat16

D_MODEL = 1024
N_MEM = 256
N_XHEADS = 4
XHEAD_DIM = D_MODEL // N_XHEADS
D_CONV = 512
CONV_WIDTH = 31
CONV_BUF = CONV_WIDTH - 1
CONV_PAD = 32
D_RWKV = 1024
RWKV_HEAD = 64
RWKV_HEADS = D_RWKV // RWKV_HEAD
D_LORA_WA = 128
D_GATE_LORA = 128
D_FF = 2816
EPS_RMS = 1e-6
EPS_LN = 1e-5
EPS_GN = RWKV_HEAD * 1e-5
D_SHIFT = 3 * D_RWKV + D_LORA_WA + D_GATE_LORA
OFF_RWKV = 2 * D_CONV
OFF_GATE = OFF_RWKV + D_SHIFT
D_IN = OFF_GATE + 2 * D_MODEL

LANE_GROUP = 256
HEADS_PER_GROUP = LANE_GROUP // RWKV_HEAD
N_GROUPS = D_RWKV // LANE_GROUP
WKV_CHUNK = 64
VMEM_LIMIT = 56 * 1024 * 1024


def _mm(a, b):
    return jnp.dot(a.astype(BF16), b.astype(BF16), preferred_element_type=F32)


def _mm_nt(a, b):
    return lax.dot_general(a.astype(BF16), b.astype(BF16), (((1,), (1,)), ((), ())), preferred_element_type=F32)


def _mm_tn(a, b):
    return lax.dot_general(a.astype(BF16), b.astype(BF16), (((0,), (0,)), ((), ())), preferred_element_type=F32)


def _sigmoid(x):
    return 0.5 * jnp.tanh(0.5 * x) + 0.5


def _rms(x, g):
    return x * lax.rsqrt(jnp.mean(x * x, -1, keepdims=True) + EPS_RMS) * g


def _head_sum(x, ones):
    parts = [_mm(x[:, g * LANE_GROUP:(g + 1) * LANE_GROUP], ones) for g in range(x.shape[1] // LANE_GROUP)]
    return parts[0] if len(parts) == 1 else jnp.concatenate(parts, axis=1)


def _const_spec(shape):
    nd = len(shape)
    return pl.BlockSpec(shape, lambda *_: (0,) * nd, pipeline_mode=pl.Buffered(1))


def _mem_kv_kernel(mem_ref, g_ref, wk_ref, wv_ref, mk_ref, mv_ref):
    mn = _rms(mem_ref[0], g_ref[...]).astype(BF16)
    mk_ref[0] = jnp.dot(mn, wk_ref[...], preferred_element_type=F32)
    mv_ref[0] = jnp.dot(mn, wv_ref[...], preferred_element_type=F32)


def _mem_kv(mem, g_mem, wk, wv):
    B = mem.shape[0]
    blk = pl.BlockSpec((1, N_MEM, D_MODEL), lambda b: (b, 0, 0))
    return pl.pallas_call(
        _mem_kv_kernel,
        grid=(B,),
        in_specs=[blk, _const_spec((1, D_MODEL)), _const_spec((D_MODEL, D_MODEL)), _const_spec((D_MODEL, D_MODEL))],
        out_specs=[blk, blk],
        out_shape=[jax.ShapeDtypeStruct((B, N_MEM, D_MODEL), F32)] * 2,
        compiler_params=pltpu.CompilerParams(dimension_semantics=("arbitrary",), vmem_limit_bytes=VMEM_LIMIT),
        name="mem_kv",
    )(mem, g_mem, wk, wv)


def _mix_pre_kernel(x_ref, cst_ref, sst_ref, gmix_ref, win_ref, mu_ref, wdw_ref, bdw_ref, lng_ref, lnb_ref, woc_ref,
                    w0_ref, wup_ref, a0_ref, aup_ref, gup_ref, kk_ref, ka_ref, rk_ref, ones_ref,
                    r_out, lw_out, k_out, v_out, kkn_out, b_out, g_out, bonus_out, gc_out, g2_out,
                    nconv_out, nshift_out, ubuf, sbuf, cbuf, prbuf, *, tb):
    t = pl.program_id(1)

    @pl.when(t == 0)
    def _():
        ubuf[0:CONV_PAD, :] = cst_ref[0]
        prbuf[0:8, :] = jnp.broadcast_to(sst_ref[0], (8, D_SHIFT))

    x = x_ref[0]
    n = _rms(x, gmix_ref[...]).astype(BF16)

    cin = jnp.dot(n, win_ref[:, 0:OFF_RWKV], preferred_element_type=F32)
    ubuf[CONV_PAD:CONV_PAD + tb, :] = cin[:, :D_CONV] * _sigmoid(cin[:, D_CONV:])
    base = CONV_PAD - CONV_BUF
    span = tb + CONV_PAD - 8
    for s in range(1, 8):
        sbuf[s - 1] = ubuf[s:s + span, :]
    rows = min(tb, 32)
    n_blocks = tb // rows
    pr_chunk = 256
    n_pr_chunks = D_SHIFT // pr_chunk
    pr_done = 0
    for i in range(n_blocks):
        acc = jnp.broadcast_to(bdw_ref[...], (rows // 8, 8, D_CONV))
        for j in range(CONV_WIDTH):
            q, s = divmod(j + base, 8)
            off = 8 * q + i * rows
            tap = ubuf[off:off + rows, :] if s == 0 else sbuf[s - 1, off:off + rows, :]
            acc = acc + wdw_ref[8 * j:8 * j + 8, :][None] * tap.reshape(rows // 8, 8, D_CONV)
        cbuf[i * rows:(i + 1) * rows, :] = acc.reshape(rows, D_CONV)
        pr_upto = (i + 1) * n_pr_chunks // n_blocks
        for ch in range(pr_done, pr_upto):
            c0 = ch * pr_chunk
            prbuf[8:8 + tb, c0:c0 + pr_chunk] = jnp.dot(n, win_ref[:, OFF_RWKV + c0:OFF_RWKV + c0 + pr_chunk],
                                                        preferred_element_type=F32)
        pr_done = pr_upto
    c = cbuf[...]
    mu = jnp.mean(c, -1, keepdims=True)
    cc = c - mu
    var = jnp.mean(cc * cc, -1, keepdims=True)
    c = cc * lax.rsqrt(var + EPS_LN) * lng_ref[...] + lnb_ref[...]
    conv_out = _mm(c * _sigmoid(c), woc_ref[...])
    nconv_out[0] = ubuf[tb + base: tb + CONV_PAD, :]
    ubuf[0:CONV_PAD, :] = ubuf[tb: tb + CONV_PAD, :]

    n_gate_chunks = 2 * D_MODEL // pr_chunk
    pm_parts = []
    for ch in range(n_pr_chunks):
        cs = slice(ch * pr_chunk, (ch + 1) * pr_chunk)
        pr_c = prbuf[8:8 + tb, cs]
        pm_parts.append(pr_c + (prbuf[7:7 + tb, cs] - pr_c) * mu_ref[:, cs])
        if ch < n_gate_chunks:
            gs = slice(OFF_GATE + ch * pr_chunk, OFF_GATE + (ch + 1) * pr_chunk)
            gate = _sigmoid(jnp.dot(n, win_ref[:, gs], preferred_element_type=F32))
            if ch < n_gate_chunks // 2:
                gc_out[0, :, cs] = (gate * conv_out[:, cs]).astype(BF16)
            else:
                g2_out[0, :, ch * pr_chunk - D_MODEL:(ch + 1) * pr_chunk - D_MODEL] = gate.astype(BF16)
    pm = jnp.concatenate(pm_parts, axis=1)
    last = prbuf[7 + tb:8 + tb, :]
    nshift_out[0] = last
    prbuf[7:8, :] = last

    r = pm[:, 0:D_RWKV]
    k = pm[:, D_RWKV:2 * D_RWKV]
    v = pm[:, 2 * D_RWKV:3 * D_RWKV]
    wa = pm[:, 3 * D_RWKV:3 * D_RWKV + D_LORA_WA]
    gd = pm[:, 3 * D_RWKV + D_LORA_WA:D_SHIFT]
    z = w0_ref[...] + _mm(jnp.tanh(wa), wup_ref[...])
    lw_out[0] = F32(-math.exp(-0.5)) * _sigmoid(z)
    a = _sigmoid(a0_ref[...] + _mm(wa, aup_ref[...]))
    g_out[0] = _mm(_sigmoid(gd), gup_ref[...]).astype(BF16)
    ones = ones_ref[...]
    kk = k * kk_ref[...]
    kk = kk * jnp.minimum(lax.rsqrt(_head_sum(kk * kk, ones)), 1e12)
    kf = k * (1.0 + (a - 1.0) * ka_ref[...])
    r_out[0] = r.astype(BF16)
    k_out[0] = kf.astype(BF16)
    v_out[0] = v.astype(BF16)
    kkn_out[0] = kk.astype(BF16)
    b_out[0] = (kk * a).astype(BF16)
    bonus_out[0] = _head_sum(r * kf * rk_ref[...], ones) * v


def _mix_pre(x, conv_state, shift_state, w, tb):
    B, T, D = x.shape
    tok = lambda width: pl.BlockSpec((1, tb, width), lambda b, t: (b, t, 0))
    per_b = lambda rows, width: pl.BlockSpec((1, rows, width), lambda b, t: (b, 0, 0))
    consts = [w["g_norm_mix"], w["w_in"], w["mu_shift"], w["w_dw"], w["b_dw"], w["ln_conv_g"], w["ln_conv_b"],
              w["w_o_conv"], w["w0"], w["w_up_pad"], w["a0"], w["a_up_pad"], w["g_up"], w["k_k"], w["k_a"], w["r_k"],
              w["ones_group"]]
    f32_tok = jax.ShapeDtypeStruct((B, T, D_RWKV), F32)
    bf_tok = jax.ShapeDtypeStruct((B, T, D_RWKV), BF16)
    out_shape = [bf_tok, f32_tok, bf_tok, bf_tok, bf_tok, bf_tok, bf_tok, f32_tok, bf_tok, bf_tok,
                 jax.ShapeDtypeStruct((B, CONV_BUF, D_CONV), F32), jax.ShapeDtypeStruct((B, 1, D_SHIFT), F32)]
    out_specs = [tok(D_RWKV)] * 10 + [per_b(CONV_BUF, D_CONV), per_b(1, D_SHIFT)]
    return pl.pallas_call(
        functools.partial(_mix_pre_kernel, tb=tb),
        grid=(B, T // tb),
        in_specs=[tok(D), per_b(CONV_PAD, D_CONV), per_b(1, D_SHIFT)] + [_const_spec(c.shape) for c in consts],
        out_specs=out_specs,
        out_shape=out_shape,
        scratch_shapes=[pltpu.VMEM((CONV_PAD + tb, D_CONV), F32), pltpu.VMEM((7, tb + CONV_PAD - 8, D_CONV), F32),
                        pltpu.VMEM((tb, D_CONV), F32),
                        pltpu.VMEM((8 + tb, D_SHIFT), F32)],
        compiler_params=pltpu.CompilerParams(dimension_semantics=("arbitrary", "arbitrary"),
                                             vmem_limit_bytes=VMEM_LIMIT),
        name="mix_pre",
    )(x, conv_state, shift_state, *consts)


def _wkv_kernel(*refs, n_chunks, has_state):
    if has_state:
        r_ref, lw_ref, k_ref, v_ref, kk_ref, b_ref, z0_ref, y_ref, zout_ref, z_scr = refs
    else:
        r_ref, lw_ref, k_ref, v_ref, kk_ref, b_ref, y_ref, zout_ref, z_scr = refs
    C, G = WKV_CHUNK, LANE_GROUP
    groups = range(N_GROUPS)

    @pl.when(pl.program_id(1) == 0)
    def _():
        z_scr[...] = z0_ref[0] if has_state else jnp.zeros((N_GROUPS, G, G), F32)

    row = lax.broadcasted_iota(jnp.int32, (G, G), 0)
    col = lax.broadcasted_iota(jnp.int32, (G, G), 1)
    same_head = (row // C) == (col // RWKV_HEAD)
    eye = (row == col).astype(F32)
    trow = lax.broadcasted_iota(jnp.int32, (C, G), 0)
    tcol = lax.broadcasted_iota(jnp.int32, (C, G), 1) % C
    strict = tcol < trow
    incl = tcol <= trow
    eye_w = (tcol == trow).astype(F32)
    tri = (lax.broadcasted_iota(jnp.int32, (C, 3 * C), 1) % C
           <= lax.broadcasted_iota(jnp.int32, (C, 3 * C), 0)).astype(BF16)

    def stack(x):
        xb = x.astype(BF16)
        return jnp.where(same_head, jnp.concatenate([xb] * HEADS_PER_GROUP, axis=0), jnp.zeros((), BF16))

    def cumsum(x):
        hi = x.astype(BF16)
        mid = (x - hi.astype(F32)).astype(BF16)
        lo = (x - hi.astype(F32) - mid.astype(F32)).astype(BF16)
        return jnp.dot(tri, jnp.concatenate([hi, mid, lo], axis=0), preferred_element_type=F32)

    n_sub = 8 if n_chunks % 8 == 0 else 1

    def chunk(ci, carry):
        rows = [pl.ds(pl.multiple_of((ci * n_sub + s) * C, C), C) for s in range(n_sub)]
        load = lambda ref: [ref[0, rows[s], g * G:(g + 1) * G] for s in range(n_sub) for g in groups]
        r, lw, k, v, kk, b = (load(ref) for ref in (r_ref, lw_ref, k_ref, v_ref, kk_ref, b_ref))
        L = [cumsum(x) for x in lw]
        Lc = [x[C - 1:C, :] for x in L]
        e_neg = [jnp.exp(-x) for x in L]
        e_end = [jnp.exp(c - x) for c, x in zip(Lc, L)]
        rt = [x * jnp.exp(l) for x, l in zip(r, L)]
        kkt = [x * jnp.exp(l - w) for x, l, w in zip(kk, L, lw)]
        q2 = [jnp.concatenate([a, c], axis=0).astype(BF16) for a, c in zip(kkt, rt)]
        a_k = [_mm_nt(q, stack(x * e)) for q, x, e in zip(q2, k, e_neg)]
        a_b = [_mm_nt(q, stack(x * e)) for q, x, e in zip(q2, b, e_neg)]
        l_k = [jnp.where(strict, a[:C], 0.0) for a in a_k]
        l_b = [jnp.where(strict, a[:C], 0.0) for a in a_b]
        a_rk = [jnp.where(incl, a[C:], 0.0) for a in a_k]
        a_rb = [jnp.where(incl, a[C:], 0.0) for a in a_b]
        xinv = [eye_w - a for a in l_b]
        p = [_mm(a, stack(a)) for a in l_b]
        n_sq = C.bit_length() - 2
        for j in range(n_sq):
            bd = [stack(a) for a in p]
            if j + 1 < n_sq:
                res = [_mm(jnp.concatenate([x, a], axis=0), m) for x, a, m in zip(xinv, p, bd)]
                xinv = [x + a[:C] for x, a in zip(xinv, res)]
                p = [a[C:] for a in res]
            else:
                xinv = [x + _mm(x, m) for x, m in zip(xinv, bd)]
        bdv = [stack(x) for x in v]
        lrv = [_mm(jnp.concatenate([a, c], axis=0), m) for a, c, m in zip(l_k, a_rk, bdv)]
        ku = [_mm(x, jnp.concatenate([stack(a), stack(c[:C])], axis=1)) for x, a, c in zip(xinv, kkt, lrv)]
        ru = [_mm(a, jnp.concatenate([stack(x[:, :G]), stack(x[:, G:])], axis=1)) for a, x in zip(a_rb, ku)]
        yv = [c[C:] - x[:, G:] for c, x in zip(lrv, ru)]
        lhs1 = [jnp.concatenate([x[:, :G], c - a[:, :G]], axis=0).astype(BF16) for x, c, a in zip(ku, rt, ru)]
        lhs2 = [jnp.concatenate([x * e, c * e, eye * jnp.exp(lc)], axis=0).astype(BF16)
                for x, c, e, lc in zip(k, b, e_end, Lc)]
        z = [z_scr[g] for g in groups]
        for s in range(n_sub):
            sl = slice(s * N_GROUPS, (s + 1) * N_GROUPS)
            res = [_mm(a, x) for a, x in zip(lhs1[sl], z)]
            u = [-(a[:C] + x[:, G:]) for a, x in zip(res, ku[sl])]
            for g in groups:
                y_ref[0, rows[s], g * G:(g + 1) * G] = res[g][C:] + yv[sl][g]
            znew = [_mm_tn(a, jnp.concatenate([c.astype(F32), x, zz], axis=0))
                    for a, c, x, zz in zip(lhs2[sl], v[sl], u, z)]
            z = [jnp.where(same_head, a, 0.0) for a in znew]
        for g in groups:
            z_scr[g] = z[g]
        return carry

    if n_chunks == n_sub:
        chunk(0, 0)
    else:
        lax.fori_loop(0, n_chunks // n_sub, chunk, 0)
    zout_ref[0] = z_scr[...]


def _wkv(r, lw, k, v, kk, b, z0, tbw):
    B, T, _ = r.shape
    has_state = z0 is not None
    tok = pl.BlockSpec((1, tbw, D_RWKV), lambda bi, t: (bi, t, 0))
    zspec = pl.BlockSpec((1, N_GROUPS, LANE_GROUP, LANE_GROUP), lambda bi, t: (bi, 0, 0, 0))
    ins = [r, lw, k, v, kk, b] + ([z0] if has_state else [])
    return pl.pallas_call(
        functools.partial(_wkv_kernel, n_chunks=tbw // WKV_CHUNK, has_state=has_state),
        grid=(B, T // tbw),
        in_specs=[tok] * 6 + ([zspec] if has_state else []),
        out_specs=[tok, zspec],
        out_shape=[jax.ShapeDtypeStruct((B, T, D_RWKV), F32),
                   jax.ShapeDtypeStruct((B, N_GROUPS, LANE_GROUP, LANE_GROUP), F32)],
        scratch_shapes=[pltpu.VMEM((N_GROUPS, LANE_GROUP, LANE_GROUP), F32)],
        compiler_params=pltpu.CompilerParams(dimension_semantics=("arbitrary", "arbitrary"),
                                             vmem_limit_bytes=VMEM_LIMIT),
        name="wkv",
    )(*ins)


def _mix_post_kernel(x_ref, y_ref, bonus_ref, g_ref, gc_ref, g2_ref, mk_ref, mv_ref, lnxg_ref, lnxb_ref, ones_ref,
                     worw_ref, wout_ref, gnx_ref, wq_ref, wox_ref, h_ref):
    ones = ones_ref[...]
    y = y_ref[0]
    inv_n = F32(1.0 / RWKV_HEAD)
    yc = y - _head_sum(y, ones) * inv_n
    var = _head_sum(yc * yc, ones) * inv_n
    yn = yc * lax.rsqrt(var + EPS_GN) * lnxg_ref[...] + lnxb_ref[...] + bonus_ref[0]
    rwkv_out = _mm(yn * g_ref[0].astype(F32), worw_ref[...])
    merged = gc_ref[0].astype(F32) + g2_ref[0].astype(F32) * rwkv_out
    h = x_ref[0] + _mm(merged, wout_ref[...])

    q = _mm(_rms(h, gnx_ref[...]), wq_ref[...])
    scale = F32(XHEAD_DIM ** -0.5)
    cols = [slice(hd * XHEAD_DIM, (hd + 1) * XHEAD_DIM) for hd in range(N_XHEADS)]
    s = [_mm_nt(q[:, cs], mk_ref[0, :, cs]) * scale for cs in cols]
    e = [jnp.exp(a - jnp.max(a, -1, keepdims=True)) for a in s]
    pw = [a * (1.0 / jnp.sum(a, -1, keepdims=True)) for a in e]
    o = [_mm(a, mv_ref[0, :, cs]).astype(BF16) for a, cs in zip(pw, cols)]
    h_ref[0] = h + jnp.dot(jnp.concatenate(o, axis=1), wox_ref[...], preferred_element_type=F32)


def _mix_post(x, y, bonus, g, gc, g2, mk, mv, w, tb):
    B, T, D = x.shape
    tok = pl.BlockSpec((1, tb, D), lambda b, t: (b, t, 0))
    mem = pl.BlockSpec((1, N_MEM, D), lambda b, t: (b, 0, 0))
    consts = [w["ln_x_g"], w["ln_x_b"], w["ones_group"], w["w_o_rwkv"], w["w_out"], w["g_norm_x"], w["w_q_x"],
              w["w_o_x"]]
    return pl.pallas_call(
        _mix_post_kernel,
        grid=(B, T // tb),
        in_specs=[tok] * 6 + [mem, mem] + [_const_spec(c.shape) for c in consts],
        out_specs=tok,
        out_shape=jax.ShapeDtypeStruct((B, T, D), F32),
        compiler_params=pltpu.CompilerParams(dimension_semantics=("arbitrary", "arbitrary"),
                                             vmem_limit_bytes=VMEM_LIMIT),
        name="mix_post",
    )(x, y, bonus, g, gc, g2, mk, mv, *consts)


def _ffn_kernel(h_ref, gffn_ref, wg_ref, wu_ref, wd_ref, gfin_ref, y_ref):
    h = h_ref[...]
    xn = _rms(h, gffn_ref[...]).astype(BF16)
    gate = jnp.dot(xn, wg_ref[...], preferred_element_type=F32)
    up = jnp.dot(xn, wu_ref[...], preferred_element_type=F32)
    h = h + _mm(gate * _sigmoid(gate) * up, wd_ref[...])
    y_ref[...] = _rms(h, gfin_ref[...])


def _ffn(h, w, tb):
    n_tok, D = h.shape
    tok = pl.BlockSpec((tb, D), lambda i: (i, 0))
    consts = [w["g_norm_ffn"], w["w_ffn_gate"], w["w_ffn_up"], w["w_ffn_down"], w["g_norm_final"]]
    return pl.pallas_call(
        _ffn_kernel,
        grid=(n_tok // tb,),
        in_specs=[tok] + [_const_spec(c.shape) for c in consts],
        out_specs=tok,
        out_shape=jax.ShapeDtypeStruct((n_tok, D), F32),
        compiler_params=pltpu.CompilerParams(dimension_semantics=("arbitrary",), vmem_limit_bytes=VMEM_LIMIT),
        name="ffn",
    )(h, *consts)


def _state_to_blockdiag(s):
    B = s.shape[0]
    st = jnp.swapaxes(s, -1, -2).reshape(B, N_GROUPS, HEADS_PER_GROUP, RWKV_HEAD, RWKV_HEAD)
    eye = jnp.eye(HEADS_PER_GROUP, dtype=s.dtype)
    z = st[:, :, :, :, None, :] * eye[None, None, :, None, :, None]
    return z.reshape(B, N_GROUPS, LANE_GROUP, LANE_GROUP)


def _blockdiag_to_state(z):
    B = z.shape[0]
    z5 = z.reshape(B, N_GROUPS, HEADS_PER_GROUP, RWKV_HEAD, HEADS_PER_GROUP, RWKV_HEAD)
    idx = jnp.arange(HEADS_PER_GROUP)
    blocks = z5[:, :, idx, :, idx, :]
    blocks = jnp.moveaxis(blocks, 0, 2)
    return jnp.swapaxes(blocks, -1, -2).reshape(B, RWKV_HEADS, RWKV_HEAD, RWKV_HEAD)


def _layer(x, mk, mv, conv_state, shift_state, wkv_state, w, tb, tb_ffn):
    B, T, D = x.shape
    conv_pad = jnp.pad(conv_state, ((0, 0), (CONV_PAD - CONV_BUF, 0), (0, 0)))
    (r, lw, k, v, kk, b, g, bonus, gc, g2, new_conv, new_shift) = _mix_pre(x, conv_pad, shift_state, w, tb)
    t_pad = -T % WKV_CHUNK
    ops = [r, lw, k, v, kk, b]
    if t_pad:
        ops = [jnp.pad(o, ((0, 0), (0, t_pad), (0, 0))) for o in ops]
    tbw = min(T + t_pad, 512)
    z0 = None if wkv_state is None else _state_to_blockdiag(wkv_state)
    y, z = _wkv(*ops, z0, tbw)
    if t_pad:
        y = y[:, :T]
    h = _mix_post(x, y, bonus, g, gc, g2, mk, mv, w, min(T, 2 * tb))
    out = _ffn(h.reshape(B * T, D), w, tb_ffn).reshape(B, T, D)
    return out, new_conv, new_shift, _blockdiag_to_state(z)


def kernel(x_prompt, x_sample, cache_mem_k, cache_mem_v, state_conv, state_shift, state_wkv, mem_prompt, g_norm_mix, w_in, mu_shift, w_dw, b_dw, ln_conv_g, ln_conv_b, w_o_conv, w0, w_up, a0, a_up, g_up, k_k, k_a, r_k, ln_x_g, ln_x_b, w_o_rwkv, w_out, g_norm_x, g_mem, w_q_x, w_k_mem, w_v_mem, w_o_x, g_norm_ffn, w_ffn_gate, w_ffn_up, w_ffn_down, g_norm_final):
    Bp, Tp, D = x_prompt.shape
    Bs, Ts, _ = x_sample.shape
    row = lambda a: a.reshape(1, -1)
    zeros_lora = jnp.zeros((D_LORA_WA // 2, D_RWKV), BF16)
    head_of = jnp.arange(LANE_GROUP) // RWKV_HEAD
    w = {
        "g_norm_mix": row(g_norm_mix[0]), "w_in": w_in[0].astype(BF16), "mu_shift": row(mu_shift[0]),
        "w_dw": jnp.repeat(w_dw[0], 8, axis=0), "b_dw": row(b_dw[0]), "ln_conv_g": row(ln_conv_g[0]), "ln_conv_b": row(ln_conv_b[0]),
        "w_o_conv": w_o_conv[0].astype(BF16), "w0": row(w0[0]),
        "w_up_pad": jnp.concatenate([w_up[0].astype(BF16), zeros_lora], 0), "a0": row(a0[0]),
        "a_up_pad": jnp.concatenate([zeros_lora, a_up[0].astype(BF16)], 0), "g_up": g_up[0].astype(BF16),
        "k_k": row(k_k[0]), "k_a": row(k_a[0]), "r_k": row(r_k[0]),
        "ones_group": (head_of[:, None] == head_of[None, :]).astype(BF16),
        "ln_x_g": row(ln_x_g[0]), "ln_x_b": row(ln_x_b[0]), "w_o_rwkv": w_o_rwkv[0].astype(BF16),
        "w_out": w_out[0].astype(BF16), "g_norm_x": row(g_norm_x[0]), "w_q_x": w_q_x[0].astype(BF16),
        "w_o_x": w_o_x[0].astype(BF16), "g_norm_ffn": row(g_norm_ffn[0]),
        "w_ffn_gate": w_ffn_gate[0].astype(BF16), "w_ffn_up": w_ffn_up[0].astype(BF16),
        "w_ffn_down": w_ffn_down[0].astype(BF16), "g_norm_final": row(g_norm_final),
    }
    mk, mv = _mem_kv(mem_prompt, row(g_mem[0]), w_k_mem[0].astype(BF16), w_v_mem[0].astype(BF16))
    conv0 = jnp.zeros((Bp, CONV_BUF, D_CONV), F32)
    shift0 = jnp.zeros((Bp, 1, D_SHIFT), F32)
    tb_p = min(Tp, 256)
    yp, cp, sp, wp = _layer(x_prompt, mk.astype(BF16), mv.astype(BF16), conv0, shift0, None, w, tb_p,
                            min(Bp * Tp, 512))
    mk_s = cache_mem_k[0].reshape(Bs, N_MEM, D).astype(BF16)
    mv_s = cache_mem_v[0].reshape(Bs, N_MEM, D).astype(BF16)
    ys, cs, ss, ws = _layer(x_sample, mk_s, mv_s, state_conv[0], state_shift[0], state_wkv[0], w, Ts,
                            min(Bs * Ts, 512))
    mem_shape = (1, Bp, N_MEM, N_XHEADS, XHEAD_DIM)
    return (yp, ys, mk.reshape(mem_shape), mv.reshape(mem_shape), cp[None], sp[None], wp[None],
            cs[None], ss[None], ws[None])
```
